```python
import math
import jax
import jax.numpy as jnp
from jax import lax
import numpy as np

D_MODEL = 1024
BATCH = 8
SEQ = 2048
DEPTH = 2
DEC_BATCH = 4
DEC_SEQ = 8192
PAST_LEN = 128

MIX_WIDTH = D_MODEL
S5_WIDTH = MIX_WIDTH // 4
S5_GROUP = 16
S5_GROUPS = S5_WIDTH // S5_GROUP
S5_STATE = 64
FNET_WIDTH = MIX_WIDTH // 4
FNET_GROUPS = 4
FNET_GROUP = FNET_WIDTH // FNET_GROUPS
SSD_WIDTH = MIX_WIDTH - S5_WIDTH - FNET_WIDTH
SSD_HEAD_DIM = 64
SSD_HEADS = SSD_WIDTH // SSD_HEAD_DIM
SSD_GROUPS = 2
SSD_HEADS_PER_GROUP = SSD_HEADS // SSD_GROUPS
SSD_STATE = 128
SSD_CONV = 5
SSD_CHUNK = 128
SSD_BC = SSD_GROUPS * SSD_STATE
SSD_CONV_DIM = SSD_WIDTH + 2 * SSD_BC
SSD_PROJ = SSD_WIDTH + SSD_CONV_DIM + SSD_HEADS
IN_PROJ = S5_WIDTH + FNET_WIDTH + SSD_PROJ
D_FF = int(math.ceil(8 * D_MODEL / 3 / 256)) * 256
EPS = 1e-6

kernel_name = 'hybrid_s5_fnet_ssd_encoder'


def rmsnorm(x, g):
    xf = x.astype(jnp.float32)
    y = xf * lax.rsqrt(jnp.mean(xf * xf, axis=-1, keepdims=True) + EPS)
    return (y * g.astype(jnp.float32)).astype(x.dtype)


def s5_direction(u, b_re, b_im, lam_re, lam_im, log_step, c_re, c_im, reverse):
    lam = lax.complex(lam_re.astype(jnp.float32), lam_im.astype(jnp.float32))
    step = jnp.exp(log_step.astype(jnp.float32))[:, None]
    lam_bar = jnp.exp(lam * step)
    b = lax.complex(b_re.astype(jnp.float32), b_im.astype(jnp.float32))
    b_bar = ((lam_bar - 1.0) / lam)[..., None] * b
    bu = jnp.einsum('gpc,blgc->blgp', b_bar, u.astype(jnp.complex64))
    a = jnp.broadcast_to(lam_bar, bu.shape)

    def combine(e1, e2):
        a1, h1 = e1
        a2, h2 = e2
        return a1 * a2, a2 * h1 + h2

    _, h = lax.associative_scan(combine, (a, bu), axis=1, reverse=reverse)
    c = lax.complex(c_re.astype(jnp.float32), c_im.astype(jnp.float32))
    return jnp.einsum('gcp,blgp->blgc', c, h).real


def s5_mixer(u, b_re, b_im, lam_re_f, lam_im_f, log_step_f, c_re_f, c_im_f,
             lam_re_b, lam_im_b, log_step_b, c_re_b, c_im_b, d, w_glu, b_glu):
    bsz, l, _ = u.shape
    ug = u.astype(jnp.float32).reshape(bsz, l, S5_GROUPS, S5_GROUP)
    y = (s5_direction(ug, b_re, b_im, lam_re_f, lam_im_f, log_step_f, c_re_f, c_im_f, False)
         + s5_direction(ug, b_re, b_im, lam_re_b, lam_im_b, log_step_b, c_re_b, c_im_b, True))
    y = y.reshape(bsz, l, S5_WIDTH) + ug.reshape(bsz, l, S5_WIDTH) * d.astype(jnp.float32)
    g = jax.nn.gelu(y)
    return g * jax.nn.sigmoid(g @ w_glu.astype(jnp.float32) + b_glu.astype(jnp.float32))


def fnet_mixer(v, w, b):
    bsz, l, _ = v.shape
    vg = v.astype(jnp.float32).reshape(bsz, l, FNET_GROUPS, FNET_GROUP)
    f = jnp.fft.fftn(vg, axes=(1, 3), norm='ortho').real
    y = jnp.einsum('blgc,gcd->blgd', f, w.astype(jnp.float32)) + b.astype(jnp.float32)
    return y.reshape(bsz, l, FNET_WIDTH)


def segsum(a):
    t = a.shape[-1]
    a_rep = jnp.broadcast_to(a[..., :, None], a.shape + (t,))
    strict = jnp.tril(jnp.ones((t, t), dtype=bool), -1)
    seg = jnp.cumsum(jnp.where(strict, a_rep, 0.0), axis=-2)
    return jnp.where(jnp.tril(jnp.ones((t, t), dtype=bool)), seg, -jnp.inf)


def ssd_scan(x, dt, a, bm, cm):
    bsz, l = x.shape[:2]
    nc = l // SSD_CHUNK
    t = SSD_CHUNK
    g, r = SSD_GROUPS, SSD_HEADS_PER_GROUP
    xd = (x * dt[..., None]).reshape(bsz, nc, t, g, r, SSD_HEAD_DIM)
    da = (dt * a).reshape(bsz, nc, t, g, r).transpose(0, 3, 4, 1, 2)
    bc = bm.reshape(bsz, nc, t, g, SSD_STATE)
    cc = cm.reshape(bsz, nc, t, g, SSD_STATE)
    a_cs = jnp.cumsum(da, axis=-1)
    lmat = jnp.exp(segsum(da))
    cb = jnp.einsum('bclgn,bcsgn->bcgls', cc, bc)
    y_diag = jnp.einsum('bcgls,bgrcls,bcsgrp->bclgrp', cb, lmat, xd)
    decay_states = jnp.exp(a_cs[..., -1:] - a_cs)
    states = jnp.einsum('bclgn,bgrcl,bclgrp->bcgrpn', bc, decay_states, xd)
    states = jnp.concatenate([jnp.zeros_like(states[:, :1]), states], axis=1)
    chunk_tot = jnp.pad(a_cs[..., -1], ((0, 0), (0, 0), (0, 0), (1, 0)))
    decay_chunk = jnp.exp(segsum(chunk_tot))
    states = jnp.einsum('bgrzc,bcgrpn->bzgrpn', decay_chunk, states)[:, :-1]
    y_off = jnp.einsum('bclgn,bcgrpn,bgrcl->bclgrp', cc, states, jnp.exp(a_cs))
    return (y_diag + y_off).reshape(bsz, l, SSD_HEADS, SSD_HEAD_DIM)


def centred_depthwise_conv(x, w, b):
    pad = (SSD_CONV - 1) // 2
    y = lax.conv_general_dilated(x, w[:, None, :], window_strides=(1,), padding=[(pad, pad)],
                                 dimension_numbers=('NWC', 'WIO', 'NWC'),
                                 feature_group_count=x.shape[-1])
    return y + b


def ssd_mixer(p, conv_w, conv_b, a_log_f, dt_bias_f, a_log_b, dt_bias_b, d, norm_g):
    bsz, l, _ = p.shape
    p = p.astype(jnp.float32)
    z = p[..., :SSD_WIDTH]
    xbc = p[..., SSD_WIDTH:SSD_WIDTH + SSD_CONV_DIM]
    dt_raw = p[..., SSD_WIDTH + SSD_CONV_DIM:]
    xbc = jax.nn.silu(centred_depthwise_conv(xbc, conv_w.astype(jnp.float32), conv_b.astype(jnp.float32)))
    x = xbc[..., :SSD_WIDTH].reshape(bsz, l, SSD_HEADS, SSD_HEAD_DIM)
    bm = xbc[..., SSD_WIDTH:SSD_WIDTH + SSD_BC].reshape(bsz, l, SSD_GROUPS, SSD_STATE)
    cm = xbc[..., SSD_WIDTH + SSD_BC:].reshape(bsz, l, SSD_GROUPS, SSD_STATE)
    dt_f = jax.nn.softplus(dt_raw + dt_bias_f.astype(jnp.float32))
    dt_b = jax.nn.softplus(dt_raw + dt_bias_b.astype(jnp.float32))
    a_f = -jnp.exp(a_log_f.astype(jnp.float32))
    a_b = -jnp.exp(a_log_b.astype(jnp.float32))
    y_f = ssd_scan(x, dt_f, a_f, bm, cm)
    y_b = jnp.flip(ssd_scan(jnp.flip(x, 1), jnp.flip(dt_b, 1), a_b, jnp.flip(bm, 1), jnp.flip(cm, 1)), 1)
    y = y_f + y_b + x * d.astype(jnp.float32)[:, None]
    y = y.reshape(bsz, l, SSD_WIDTH) * jax.nn.silu(z)
    return rmsnorm(y, norm_g)


def trunk(x, norm_mix_g, w_in, s5_b_re, s5_b_im,
          s5_lam_re_f, s5_lam_im_f, s5_log_step_f, s5_c_re_f, s5_c_im_f,
          s5_lam_re_b, s5_lam_im_b, s5_log_step_b, s5_c_re_b, s5_c_im_b,
          s5_d, s5_w_glu, s5_b_glu, fnet_w, fnet_b,
          ssd_conv_w, ssd_conv_b, ssd_a_log_f, ssd_dt_bias_f, ssd_a_log_b, ssd_dt_bias_b,
          ssd_d, ssd_norm_g, w_out, norm_ffn_g, w_gate, w_up, w_down, final_norm_g):
    for i in range(DEPTH):
        h = rmsnorm(x, norm_mix_g[i])
        proj = h @ w_in[i]
        u = proj[..., :S5_WIDTH]
        v = proj[..., S5_WIDTH:S5_WIDTH + FNET_WIDTH]
        p = proj[..., S5_WIDTH + FNET_WIDTH:]
        y_a = s5_mixer(u, s5_b_re[i], s5_b_im[i],
                       s5_lam_re_f[i], s5_lam_im_f[i], s5_log_step_f[i], s5_c_re_f[i], s5_c_im_f[i],
                       s5_lam_re_b[i], s5_lam_im_b[i], s5_log_step_b[i], s5_c_re_b[i], s5_c_im_b[i],
                       s5_d[i], s5_w_glu[i], s5_b_glu[i])
        y_b = fnet_mixer(v, fnet_w[i], fnet_b[i])
        y_c = ssd_mixer(p, ssd_conv_w[i], ssd_conv_b[i], ssd_a_log_f[i], ssd_dt_bias_f[i],
                        ssd_a_log_b[i], ssd_dt_bias_b[i], ssd_d[i], ssd_norm_g[i])
        mix = jnp.concatenate([y_a, y_b, y_c], axis=-1).astype(x.dtype)
        x = x + mix @ w_out[i]
        h = rmsnorm(x, norm_ffn_g[i])
        x = x + (jax.nn.silu(h @ w_gate[i]) * (h @ w_up[i])) @ w_down[i]
    return rmsnorm(x, final_norm_g)


def setup_inputs(seed: int = 0) -> dict:
    key = jax.random.key(seed)
    ks = list(jax.random.split(key, 48))
    L, G, P, C = DEPTH, S5_GROUPS, S5_STATE, S5_GROUP
    n_idx = jnp.arange(P, dtype=jnp.float32)

    def nrm(i, shape, scale):
        return jax.random.normal(ks[i], shape, jnp.float32) * scale

    def unif(i, shape, lo, hi):
        return jax.random.uniform(ks[i], shape, jnp.float32, lo, hi)

    def gain(i, shape):
        return 1.0 + nrm(i, shape, 0.02)

    def dt_bias(i):
        dt = jnp.exp(unif(i, (L, SSD_HEADS), math.log(1e-3), math.log(1e-1)))
        return dt + jnp.log(-jnp.expm1(-dt))

    return {
        'x_prompt': nrm(0, (BATCH, SEQ, D_MODEL), 1.0),
        'x_sample': nrm(1, (DEC_BATCH, DEC_SEQ, D_MODEL), 1.0),
        'norm_mix_g': gain(2, (L, D_MODEL)),
        'w_in': nrm(3, (L, D_MODEL, IN_PROJ), D_MODEL ** -0.5),
        's5_b_re': nrm(4, (L, G, P, C), (2 * C) ** -0.5),
        's5_b_im': nrm(5, (L, G, P, C), (2 * C) ** -0.5),
        's5_lam_re_f': -0.5 + nrm(6, (L, G, P), 0.01),
        's5_lam_im_f': math.pi * n_idx + nrm(7, (L, G, P), 0.01),
        's5_log_step_f': unif(8, (L, G), math.log(1e-3), math.log(1e-1)),
        's5_c_re_f': nrm(9, (L, G, C, P), P ** -0.5),
        's5_c_im_f': nrm(10, (L, G, C, P), P ** -0.5),
        's5_lam_re_b': -0.5 + nrm(11, (L, G, P), 0.01),
        's5_lam_im_b': math.pi * n_idx + nrm(12, (L, G, P), 0.01),
        's5_log_step_b': unif(13, (L, G), math.log(1e-3), math.log(1e-1)),
        's5_c_re_b': nrm(14, (L, G, C, P), P ** -0.5),
        's5_c_im_b': nrm(15, (L, G, C, P), P ** -0.5),
        's5_d': nrm(16, (L, S5_WIDTH), 1.0),
        's5_w_glu': nrm(17, (L, S5_WIDTH, S5_WIDTH), S5_WIDTH ** -0.5),
        's5_b_glu': nrm(18, (L, S5_WIDTH), 0.02),
        'fnet_w': nrm(19, (L, FNET_GROUPS, FNET_GROUP, FNET_GROUP), FNET_GROUP ** -0.5),
        'fnet_b': nrm(20, (L, FNET_GROUPS, FNET_GROUP), 0.02),
        'ssd_conv_w': nrm(21, (L, SSD_CONV, SSD_CONV_DIM), SSD_CONV ** -0.5),
        'ssd_conv_b': nrm(22, (L, SSD_CONV_DIM), 0.02),
        'ssd_a_log_f': jnp.log(unif(23, (L, SSD_HEADS), 1.0, 16.0)),
        'ssd_dt_bias_f': dt_bias(24),
        'ssd_a_log_b': jnp.log(unif(25, (L, SSD_HEADS), 1.0, 16.0)),
        'ssd_dt_bias_b': dt_bias(26),
        'ssd_d': gain(27, (L, SSD_HEADS)),
        'ssd_norm_g': gain(28, (L, SSD_WIDTH)),
        'w_out': nrm(29, (L, MIX_WIDTH, D_MODEL), MIX_WIDTH ** -0.5),
        'norm_ffn_g': gain(30, (L, D_MODEL)),
        'w_gate': nrm(31, (L, D_MODEL, D_FF), D_MODEL ** -0.5),
        'w_up': nrm(32, (L, D_MODEL, D_FF), D_MODEL ** -0.5),
        'w_down': nrm(33, (L, D_FF, D_MODEL), D_FF ** -0.5),
        'final_norm_g': gain(34, (D_MODEL,)),
    }


def reference(x_prompt, x_sample, norm_mix_g, w_in, s5_b_re, s5_b_im,
              s5_lam_re_f, s5_lam_im_f, s5_log_step_f, s5_c_re_f, s5_c_im_f,
              s5_lam_re_b, s5_lam_im_b, s5_log_step_b, s5_c_re_b, s5_c_im_b,
              s5_d, s5_w_glu, s5_b_glu, fnet_w, fnet_b,
              ssd_conv_w, ssd_conv_b, ssd_a_log_f, ssd_dt_bias_f, ssd_a_log_b, ssd_dt_bias_b,
              ssd_d, ssd_norm_g, w_out, norm_ffn_g, w_gate, w_up, w_down, final_norm_g):
    y_prompt = trunk(x_prompt, norm_mix_g, w_in, s5_b_re, s5_b_im,
                     s5_lam_re_f, s5_lam_im_f, s5_log_step_f, s5_c_re_f, s5_c_im_f,
                     s5_lam_re_b, s5_lam_im_b, s5_log_step_b, s5_c_re_b, s5_c_im_b,
                     s5_d, s5_w_glu, s5_b_glu, fnet_w, fnet_b,
                     ssd_conv_w, ssd_conv_b, ssd_a_log_f, ssd_dt_bias_f, ssd_a_log_b, ssd_dt_bias_b,
                     ssd_d, ssd_norm_g, w_out, norm_ffn_g, w_gate, w_up, w_down, final_norm_g)
    y_sample = trunk(x_sample, norm_mix_g, w_in, s5_b_re, s5_b_im,
                     s5_lam_re_f, s5_lam_im_f, s5_log_step_f, s5_c_re_f, s5_c_im_f,
                     s5_lam_re_b, s5_lam_im_b, s5_log_step_b, s5_c_re_b, s5_c_im_b,
                     s5_d, s5_w_glu, s5_b_glu, fnet_w, fnet_b,
                     ssd_conv_w, ssd_conv_b, ssd_a_log_f, ssd_dt_bias_f, ssd_a_log_b, ssd_dt_bias_b,
                     ssd_d, ssd_norm_g, w_out, norm_ffn_g, w_gate, w_up, w_down, final_norm_g)
    return (y_prompt, y_sample)
```

```python
import functools
import math

import numpy as np
import jax
import jax.numpy as jnp
from jax import lax
from jax.experimental import pallas as pl
from jax.experimental.pallas import tpu as pltpu

F32 = jnp.float32
BF16 = jnp.bfloat16

D_MODEL = 1024
S5_WIDTH = 256
S5_GROUP = 16
S5_GROUPS = 16
S5_STATE = 64
S5_NSTATE = S5_GROUPS * S5_STATE
FNET_WIDTH = 256
FNET_GROUPS = 4
FNET_GROUP = 64
SSD_WIDTH = 512
SSD_HEAD_DIM = 64
SSD_HEADS = 8
SSD_GROUPS = 2
SSD_STATE = 128
SSD_CONV = 5
SSD_BC = SSD_GROUPS * SSD_STATE
SSD_CONV_DIM = SSD_WIDTH + 2 * SSD_BC
DT_PAD = 128
IN_PROJ_PAD = S5_WIDTH + FNET_WIDTH + SSD_WIDTH + SSD_CONV_DIM + DT_PAD
D_FF = 2816
EPS = 1e-6

LANES = 128
SUBLANES = 8
VMEM_LIMIT = 56 * 1024 * 1024

TOKEN_TILE = 512
S5_CHUNK = 256
S5_SEQS = 2
SSD_CHUNK = 128
CONV_CHUNK = 512
FNET_L2 = 128
FNET_KB = 8
FNET_COLS = 4096


def _rmsnorm(x, g):
    return x * lax.rsqrt(jnp.mean(x * x, axis=-1, keepdims=True) + EPS) * g


def _split3(v):
    hi = v.astype(BF16)
    r = v - hi.astype(F32)
    mid = r.astype(BF16)
    lo = (r - mid.astype(F32)).astype(BF16)
    return hi, mid, lo


def _dot(a, b):
    return jnp.dot(a, b, preferred_element_type=F32)


def _dot_exact_lhs(m, v):
    hi, mid, lo = _split3(v)
    return _dot(m, hi) + _dot(m, mid) + _dot(m, lo)


def _dot_exact_rhs(v, m):
    hi, mid, lo = _split3(v)
    return _dot(hi, m) + _dot(mid, m) + _dot(lo, m)


def _dot_hilo(m_hi, m_lo, v):
    v_hi = v.astype(BF16)
    v_lo = (v - v_hi.astype(F32)).astype(BF16)
    return _dot(m_hi, v_hi) + _dot(m_hi, v_lo) + _dot(m_lo, v_hi)


def _const_spec(shape):
    nd = len(shape)
    return pl.BlockSpec(shape, lambda *_: (0,) * nd, pipeline_mode=pl.Buffered(1))


def _params(semantics):
    return pltpu.CompilerParams(dimension_semantics=semantics, vmem_limit_bytes=VMEM_LIMIT)


_INPROJ_SPLITS = (0, S5_WIDTH, S5_WIDTH + FNET_WIDTH, S5_WIDTH + FNET_WIDTH + SSD_WIDTH,
                  S5_WIDTH + FNET_WIDTH + SSD_WIDTH + SSD_CONV_DIM, IN_PROJ_PAD)


def _norm_inproj_kernel(x_ref, g_ref, w_ref, u_ref, v_ref, z_ref, xbc_ref, dt_ref):
    h = _rmsnorm(x_ref[...], g_ref[...]).astype(BF16)
    for o_ref, lo, hi in zip((u_ref, v_ref, z_ref, xbc_ref, dt_ref), _INPROJ_SPLITS[:-1], _INPROJ_SPLITS[1:]):
        o_ref[...] = _dot(h, w_ref[:, lo:hi])


def _norm_inproj(x, g, w):
    n = x.shape[0]
    widths = [hi - lo for lo, hi in zip(_INPROJ_SPLITS[:-1], _INPROJ_SPLITS[1:])]
    return pl.pallas_call(
        _norm_inproj_kernel,
        out_shape=[jax.ShapeDtypeStruct((n, wd), F32) for wd in widths],
        grid=(n // TOKEN_TILE,),
        in_specs=[pl.BlockSpec((TOKEN_TILE, D_MODEL), lambda i: (i, 0)),
                  _const_spec((1, D_MODEL)), _const_spec((D_MODEL, IN_PROJ_PAD))],
        out_specs=[pl.BlockSpec((TOKEN_TILE, wd), lambda i: (i, 0)) for wd in widths],
        compiler_params=_params(("parallel",)),
        name="norm_inproj",
    )(x, g, w)


def _s5_scan_kernel(uf_ref, ub_ref, bmat_ref, cmat_ref, lam_ref, yf_ref, yb_ref, hbuf, carry):
    c = pl.program_id(1)
    nseq, t_len = uf_ref.shape[0], uf_ref.shape[1]
    chains = [(i, d) for i in range(nseq) for d in range(2)]

    @pl.when(c == 0)
    def _():
        carry[...] = jnp.zeros_like(carry)

    for i, d in chains:
        u16 = (uf_ref if d == 0 else ub_ref)[i].astype(BF16)
        for part in range(2):
            col = (2 * d + part) * S5_NSTATE
            bu = _dot(u16, bmat_ref[:, col:col + S5_NSTATE])
            k = (i * 2 + d) * 2 + part
            for j in range(SUBLANES):
                hbuf[k, pl.ds(j, t_len, stride=SUBLANES), :] = bu[:, j * LANES:(j + 1) * LANES]

    lam = [lam_ref[k] for k in range(4)]

    def step(t, hs):
        out = []
        for n, (i, d) in enumerate(chains):
            kre = (i * 2 + d) * 2
            row = pl.multiple_of((t if d == 0 else t_len - 1 - t) * SUBLANES, SUBLANES)
            a_re, a_im = lam[2 * d], lam[2 * d + 1]
            h_re, h_im = hs[2 * n], hs[2 * n + 1]
            n_re = a_re * h_re - a_im * h_im + hbuf[kre, pl.ds(row, SUBLANES), :]
            n_im = a_re * h_im + a_im * h_re + hbuf[kre + 1, pl.ds(row, SUBLANES), :]
            hbuf[kre, pl.ds(row, SUBLANES), :] = n_re
            hbuf[kre + 1, pl.ds(row, SUBLANES), :] = n_im
            out += [n_re, n_im]
        return tuple(out)

    hs = lax.fori_loop(0, t_len, step, tuple(carry[k] for k in range(4 * nseq)), unroll=8)
    for k in range(4 * nseq):
        carry[k] = hs[k]

    for i, d in chains:
        kre = (i * 2 + d) * 2
        h = jnp.concatenate(
            [hbuf[kre + part, pl.ds(j, t_len, stride=SUBLANES), :] for part in range(2) for j in range(SUBLANES)],
            axis=1).astype(BF16)
        (yf_ref if d == 0 else yb_ref)[i] = _dot(h, cmat_ref[d])


def _s5_scan(u, bmat, cmat, lam):
    b, l, _ = u.shape
    nc = l // S5_CHUNK
    blk = (S5_SEQS, S5_CHUNK, S5_WIDTH)
    fwd = pl.BlockSpec(blk, lambda i, c: (i, c, 0))
    bwd = pl.BlockSpec(blk, lambda i, c: (i, nc - 1 - c, 0))
    return pl.pallas_call(
        _s5_scan_kernel,
        out_shape=[jax.ShapeDtypeStruct(u.shape, F32)] * 2,
        grid=(b // S5_SEQS, nc),
        in_specs=[fwd, bwd, _const_spec(bmat.shape), _const_spec(cmat.shape), _const_spec(lam.shape)],
        out_specs=[fwd, bwd],
        scratch_shapes=[pltpu.VMEM((S5_SEQS * 4, S5_CHUNK * SUBLANES, LANES), F32),
                        pltpu.VMEM((S5_SEQS * 4, SUBLANES, LANES), F32)],
        compiler_params=_params(("parallel", "arbitrary")),
        name="s5_scan",
    )(u, u, bmat, cmat, lam)


def _s5_prepare(b_re, b_im, dirs):
    g, p, c = S5_GROUPS, S5_STATE, S5_GROUP
    eye = jnp.eye(g, dtype=F32)
    bcols, cmats, lams = [], [], []
    for lam_re, lam_im, log_step, c_re, c_im in dirs:
        step = jnp.exp(log_step)[:, None]
        mag = jnp.exp(lam_re * step)
        lb_re, lb_im = mag * jnp.cos(lam_im * step), mag * jnp.sin(lam_im * step)
        den = lam_re * lam_re + lam_im * lam_im
        q_re = ((lb_re - 1.0) * lam_re + lb_im * lam_im) / den
        q_im = (lb_im * lam_re - (lb_re - 1.0) * lam_im) / den
        bb_re = q_re[..., None] * b_re - q_im[..., None] * b_im
        bb_im = q_re[..., None] * b_im + q_im[..., None] * b_re
        for bb in (bb_re, bb_im):
            bcols.append(jnp.einsum('gpc,gh->gchp', bb, eye).reshape(g * c, g * p))
        cm = [jnp.einsum('gcp,gh->gphc', cc, eye).reshape(g * p, g * c) for cc in (c_re, -c_im)]
        cmats.append(jnp.concatenate(cm, axis=0))
        lams += [lb_re.reshape(SUBLANES, LANES), lb_im.reshape(SUBLANES, LANES)]
    return (jnp.concatenate(bcols, axis=1).astype(BF16), jnp.stack(cmats).astype(BF16), jnp.stack(lams))


def _dft_parts(n, scale):
    k = np.arange(n)
    ang = 2.0 * np.pi * ((k[:, None] * k[None, :]) % n) / n
    return np.cos(ang) * scale, np.sin(ang) * scale


def _hilo(m):
    m = np.asarray(m, np.float32)
    hi = jnp.asarray(m, F32).astype(BF16)
    lo = (jnp.asarray(m, F32) - hi.astype(F32)).astype(BF16)
    return hi, lo


def _fnet_fold_kernel(ch_ref, cl_ref, sh_ref, sl_ref, w_ref, a_ref, b_ref):
    w = w_ref[...]
    a_ref[...] = _dot_hilo(ch_ref[...], cl_ref[...], w).astype(BF16)
    b_ref[...] = _dot_hilo(sh_ref[...], sl_ref[...], w).astype(BF16)


def _fnet_fold(w):
    c64, s64 = _dft_parts(FNET_GROUP, 1.0 / math.sqrt(FNET_GROUP))
    eye = np.eye(FNET_GROUPS)
    cbd, sbd = np.kron(eye, c64), np.kron(eye, s64)
    wbd = jnp.einsum('gcd,gh->gchd', w, jnp.eye(FNET_GROUPS, dtype=F32)).reshape(FNET_WIDTH, FNET_WIDTH)
    shape = (FNET_WIDTH, FNET_WIDTH)
    return pl.pallas_call(
        _fnet_fold_kernel,
        out_shape=[jax.ShapeDtypeStruct(shape, BF16)] * 2,
        name="fnet_fold",
    )(*_hilo(cbd), *_hilo(sbd), wbd)


def _fnet_stage1_kernel(x_ref, fh_ref, fl_ref, gre_ref, gim_ref):
    l1 = x_ref.shape[1]
    g = _dot_hilo(fh_ref[...], fl_ref[...], x_ref[0])
    gre_ref[0] = g[:l1]
    gim_ref[0] = g[l1:]


def _fnet_stage2_kernel(gre_ref, gim_ref, twr_ref, twi_ref, fh_ref, fl_ref, a_ref, b_ref, bias_ref, o_ref):
    l2 = gre_ref.shape[2]
    for i in range(gre_ref.shape[1]):
        gr, gi = gre_ref[0, i], gim_ref[0, i]
        tr = jnp.concatenate([twr_ref[i]] * (FNET_WIDTH // LANES), axis=1)
        ti = jnp.concatenate([twi_ref[i]] * (FNET_WIDTH // LANES), axis=1)
        s = jnp.concatenate([gr * tr - gi * ti, gr * ti + gi * tr], axis=0)
        y = _dot_hilo(fh_ref[...], fl_ref[...], s)
        out = _dot(y[:l2].astype(BF16), a_ref[...]) + _dot(y[l2:].astype(BF16), b_ref[...]) + bias_ref[...]
        o_ref[0, :, i * FNET_WIDTH:(i + 1) * FNET_WIDTH] = out


def _fnet(v, a, bm, bias):
    b, l, c = v.shape
    l2 = FNET_L2
    l1 = l // l2
    c1, s1 = _dft_parts(l1, 1.0 / math.sqrt(l1))
    f1h, f1l = _hilo(np.concatenate([c1, -s1], axis=0))
    c2, s2 = _dft_parts(l2, 1.0 / math.sqrt(l2))
    f2h, f2l = _hilo(np.block([[c2, s2], [-s2, c2]]))
    m = (np.arange(l1)[:, None] * np.arange(l2)[None, :]) % l
    ang = 2.0 * np.pi * m / l
    twr = jnp.broadcast_to(jnp.asarray(np.cos(ang), F32)[:, :, None], (l1, l2, LANES))
    twi = jnp.broadcast_to(jnp.asarray(-np.sin(ang), F32)[:, :, None], (l1, l2, LANES))

    cols = min(FNET_COLS, l2 * c)
    xs = pl.BlockSpec((1, l1, cols), lambda i, j: (i, 0, j))
    gre, gim = pl.pallas_call(
        _fnet_stage1_kernel,
        out_shape=[jax.ShapeDtypeStruct((b, l1, l2 * c), F32)] * 2,
        grid=(b, l2 * c // cols),
        in_specs=[xs, _const_spec(f1h.shape), _const_spec(f1l.shape)],
        out_specs=[xs, xs],
        compiler_params=_params(("parallel", "parallel")),
        name="fnet_stage1",
    )(v.reshape(b, l1, l2 * c), f1h, f1l)

    kb = min(FNET_KB, l1)
    gs = pl.BlockSpec((1, kb, l2, c), lambda i, j: (i, j, 0, 0))
    ts = pl.BlockSpec((kb, l2, LANES), lambda i, j: (j, 0, 0))
    y = pl.pallas_call(
        _fnet_stage2_kernel,
        out_shape=jax.ShapeDtypeStruct((b, l2, l1 * c), F32),
        grid=(b, l1 // kb),
        in_specs=[gs, gs, ts, ts, _const_spec(f2h.shape), _const_spec(f2l.shape),
                  _const_spec(a.shape), _const_spec(bm.shape), _const_spec(bias.shape)],
        out_specs=pl.BlockSpec((1, l2, kb * c), lambda i, j: (i, 0, j)),
        compiler_params=_params(("parallel", "parallel")),
        name="fnet_stage2",
    )(gre.reshape(b, l1, l2, c), gim.reshape(b, l1, l2, c), twr, twi, f2h, f2l, a, bm, bias)
    return y.reshape(b, l, c)


def _ssd_conv_kernel(prev_ref, cur_ref, next_ref, w_ref, b_ref, o_ref, ext):
    c, nc = pl.program_id(1), pl.num_programs(1)
    t_len = cur_ref.shape[1]
    pad = SSD_CONV // 2
    ext[0:SUBLANES] = jnp.where(c > 0, prev_ref[0], 0.0)
    ext[SUBLANES:SUBLANES + t_len] = cur_ref[0]
    ext[SUBLANES + t_len:2 * SUBLANES + t_len] = jnp.where(c < nc - 1, next_ref[0], 0.0)
    acc = jnp.broadcast_to(b_ref[...], (t_len, SSD_CONV_DIM))
    for k in range(SSD_CONV):
        acc = acc + w_ref[k:k + 1, :] * ext[pl.ds(SUBLANES - pad + k, t_len), :]
    o_ref[0] = acc * jax.nn.sigmoid(acc)


def _ssd_conv(xbc, w, bias):
    b, l, ch = xbc.shape
    t_len = min(CONV_CHUNK, l)
    per = t_len // SUBLANES
    last = l // SUBLANES - 1
    return pl.pallas_call(
        _ssd_conv_kernel,
        out_shape=jax.ShapeDtypeStruct(xbc.shape, F32),
        grid=(b, l // t_len),
        in_specs=[pl.BlockSpec((1, SUBLANES, ch), lambda i, c: (i, jnp.maximum(c * per - 1, 0), 0)),
                  pl.BlockSpec((1, t_len, ch), lambda i, c: (i, c, 0)),
                  pl.BlockSpec((1, SUBLANES, ch), lambda i, c: (i, jnp.minimum((c + 1) * per, last), 0)),
                  _const_spec(w.shape), _const_spec(bias.shape)],
        out_specs=pl.BlockSpec((1, t_len, ch), lambda i, c: (i, c, 0)),
        scratch_shapes=[pltpu.VMEM((t_len + 2 * SUBLANES, ch), F32)],
        compiler_params=_params(("parallel", "parallel")),
        name="ssd_conv",
    )(xbc, xbc, xbc, w, bias)


def _ones_where(mask):
    return jnp.where(mask, 1.0, 0.0).astype(BF16)


def _softplus(x):
    return jnp.maximum(x, 0.0) + jnp.log1p(jnp.exp(-jnp.abs(x)))


def _ssd_scan_kernel(xf_ref, xb_ref, dtf_ref, dtb_ref, par_ref, part_ref, yf_ref, yb_ref, state):
    c = pl.program_id(1)
    t_len = xf_ref.shape[1]
    pair_w = 2 * SSD_HEAD_DIM
    heads_per_pair = pair_w // SSD_HEAD_DIM

    @pl.when(c == 0)
    def _():
        state[...] = jnp.zeros_like(state)

    row = lax.broadcasted_iota(jnp.int32, (t_len, t_len), 0)
    col = lax.broadcasted_iota(jnp.int32, (t_len, t_len), 1)
    first_head = lax.broadcasted_iota(jnp.int32, (1, pair_w), 1) < SSD_HEAD_DIM

    for d, (x_ref, dt_ref, y_ref) in enumerate(((xf_ref, dtf_ref, yf_ref), (xb_ref, dtb_ref, yb_ref))):
        dt_raw = dt_ref[0]
        dt = _softplus(dt_raw + par_ref[d:d + 1, :])
        da = dt * -jnp.exp(par_ref[2 + d:3 + d, :])
        dt_row = _softplus(dt_raw.T[0:SSD_HEADS] + part_ref[:, d:d + 1])
        da_row = dt_row * -jnp.exp(part_ref[:, 2 + d:3 + d])
        if d == 0:
            keep = row >= col
            p_col = _dot_exact_lhs(_ones_where(keep), da)
            p_row = _dot_exact_rhs(da_row, _ones_where(row <= col))
        else:
            keep = col >= row
            p_col = -_dot_exact_lhs(_ones_where(col < row), da)
            p_row = -_dot_exact_rhs(da_row, _ones_where(row < col))

        xbc = x_ref[0]
        for g in range(SSD_GROUPS):
            b_g = xbc[:, SSD_WIDTH + g * SSD_STATE:SSD_WIDTH + (g + 1) * SSD_STATE]
            c_g = xbc[:, SSD_WIDTH + SSD_BC + g * SSD_STATE:SSD_WIDTH + SSD_BC + (g + 1) * SSD_STATE].astype(BF16)
            bt_g = b_g.T.astype(BF16)
            cb = _dot(c_g, bt_g)
            pairs_per_group = SSD_HEADS // SSD_GROUPS // heads_per_pair
            for q in range(g * pairs_per_group, (g + 1) * pairs_per_group):
                h0 = q * heads_per_pair

                def expand(m):
                    return jnp.where(first_head, m[:, h0:h0 + 1], m[:, h0 + 1:h0 + 2])

                p_e = expand(p_col)
                xdt = xbc[:, q * pair_w:(q + 1) * pair_w] * expand(dt)
                if d == 0:
                    total = p_e[t_len - 1:t_len]
                    off_scale = jnp.exp(p_e)
                    w_state = jnp.exp(total - p_e)
                else:
                    total = expand(da)[t_len - 1:t_len] - p_e[t_len - 1:t_len]
                    off_scale = jnp.exp(total + p_e)
                    w_state = jnp.exp(-p_e)
                st = state[d, q]
                y = _dot(c_g, st.astype(BF16)) * off_scale
                xdt16 = xdt.astype(BF16)
                intra = []
                for h in range(h0, h0 + heads_per_pair):
                    diff = p_col[:, h:h + 1] - p_row[h:h + 1, :]
                    m = cb * jnp.exp(jnp.where(keep, diff, -jnp.inf))
                    intra.append(_dot(m.astype(BF16), xdt16))
                y_ref[0, :, q * pair_w:(q + 1) * pair_w] = y + jnp.where(first_head, intra[0], intra[1])
                state[d, q] = st * jnp.exp(total) + _dot(bt_g, (xdt * w_state).astype(BF16))


def _ssd_scan(xbc, dt, par, part):
    b, l, ch = xbc.shape
    t_len = SSD_CHUNK
    nc = l // t_len
    pairs = SSD_WIDTH // (2 * SSD_HEAD_DIM)

    def fwd(width):
        return pl.BlockSpec((1, t_len, width), lambda i, c: (i, c, 0))

    def bwd(width):
        return pl.BlockSpec((1, t_len, width), lambda i, c: (i, nc - 1 - c, 0))

    return pl.pallas_call(
        _ssd_scan_kernel,
        out_shape=[jax.ShapeDtypeStruct((b, l, SSD_WIDTH), F32)] * 2,
        grid=(b, nc),
        in_specs=[fwd(ch), bwd(ch), fwd(DT_PAD), bwd(DT_PAD), _const_spec(par.shape), _const_spec(part.shape)],
        out_specs=[fwd(SSD_WIDTH), bwd(SSD_WIDTH)],
        scratch_shapes=[pltpu.VMEM((2, pairs, SSD_STATE, 2 * SSD_HEAD_DIM), F32)],
        compiler_params=_params(("parallel", "arbitrary")),
        name="ssd_scan",
    )(xbc, xbc, dt, dt, par, part)


def _mix_out_kernel(x_ref, s5f_ref, s5b_ref, u_ref, fn_ref, sdf_ref, sdb_ref, xc_ref, z_ref,
                    s5d_ref, wglu_ref, bglu_ref, sdd_ref, sdg_ref, wo_ref, o_ref):
    ya = s5f_ref[...] + s5b_ref[...] + u_ref[...] * s5d_ref[...]
    ga = jax.nn.gelu(ya)
    ya = ga * jax.nn.sigmoid(_dot(ga.astype(BF16), wglu_ref[...]) + bglu_ref[...])
    z = z_ref[...]
    yc = (sdf_ref[...] + sdb_ref[...] + xc_ref[...] * sdd_ref[...]) * (z * jax.nn.sigmoid(z))
    yc = _rmsnorm(yc, sdg_ref[...])
    lo, mid = S5_WIDTH, S5_WIDTH + FNET_WIDTH
    o_ref[...] = (x_ref[...] + _dot(ya.astype(BF16), wo_ref[0:lo, :])
                  + _dot(fn_ref[...].astype(BF16), wo_ref[lo:mid, :])
                  + _dot(yc.astype(BF16), wo_ref[mid:, :]))


def _mix_out(x, s5f, s5b, u, fn, sdf, sdb, xbc_c, z, s5d, wglu, bglu, sdd, sdg, wo):
    n = x.shape[0]

    def rows(width):
        return pl.BlockSpec((TOKEN_TILE, width), lambda i: (i, 0))

    consts = (s5d, wglu, bglu, sdd, sdg, wo)
    return pl.pallas_call(
        _mix_out_kernel,
        out_shape=jax.ShapeDtypeStruct(x.shape, F32),
        grid=(n // TOKEN_TILE,),
        in_specs=[rows(D_MODEL), rows(S5_WIDTH), rows(S5_WIDTH), rows(S5_WIDTH), rows(FNET_WIDTH),
                  rows(SSD_WIDTH), rows(SSD_WIDTH), rows(SSD_WIDTH), rows(SSD_WIDTH)]
                 + [_const_spec(a.shape) for a in consts],
        out_specs=rows(D_MODEL),
        compiler_params=_params(("parallel",)),
        name="mix_out",
    )(x, s5f, s5b, u, fn, sdf, sdb, xbc_c, z, *consts)


def _ffn_kernel(x_ref, g_ref, wg_ref, wu_ref, wd_ref, gf_ref, o_ref, *, final_norm):
    x = x_ref[...]
    h = _rmsnorm(x, g_ref[...]).astype(BF16)
    gate = _dot(h, wg_ref[...])
    act = (gate * jax.nn.sigmoid(gate) * _dot(h, wu_ref[...])).astype(BF16)
    y = x + _dot(act, wd_ref[...])
    o_ref[...] = _rmsnorm(y, gf_ref[...]) if final_norm else y


def _ffn(x, g, wg, wu, wd, gf, final_norm):
    n = x.shape[0]
    rows = pl.BlockSpec((TOKEN_TILE, D_MODEL), lambda i: (i, 0))
    consts = (g, wg, wu, wd, gf)
    return pl.pallas_call(
        functools.partial(_ffn_kernel, final_norm=final_norm),
        out_shape=jax.ShapeDtypeStruct(x.shape, F32),
        grid=(n // TOKEN_TILE,),
        in_specs=[rows] + [_const_spec(a.shape) for a in consts],
        out_specs=rows,
        compiler_params=_params(("parallel",)),
        name="ffn",
    )(x, *consts)


def _row(v):
    return v.reshape(1, -1).astype(F32)


def _prepare_layer(i, p):
    lay = {}
    lay['norm_mix_g'] = _row(p['norm_mix_g'][i])
    lay['w_in'] = jnp.pad(p['w_in'][i], ((0, 0), (0, IN_PROJ_PAD - p['w_in'].shape[-1]))).astype(BF16)
    lay['s5'] = _s5_prepare(
        p['s5_b_re'][i], p['s5_b_im'][i],
        [(p['s5_lam_re_f'][i], p['s5_lam_im_f'][i], p['s5_log_step_f'][i], p['s5_c_re_f'][i], p['s5_c_im_f'][i]),
         (p['s5_lam_re_b'][i], p['s5_lam_im_b'][i], p['s5_log_step_b'][i], p['s5_c_re_b'][i], p['s5_c_im_b'][i])])
    lay['s5_d'] = _row(p['s5_d'][i])
    lay['s5_w_glu'] = p['s5_w_glu'][i].astype(BF16)
    lay['s5_b_glu'] = _row(p['s5_b_glu'][i])
    lay['fnet_ab'] = _fnet_fold(p['fnet_w'][i])
    lay['fnet_b'] = _row(p['fnet_b'][i])
    lay['conv_w'] = p['ssd_conv_w'][i].astype(F32)
    lay['conv_b'] = _row(p['ssd_conv_b'][i])
    par = jnp.stack([p['ssd_dt_bias_f'][i], p['ssd_dt_bias_b'][i], p['ssd_a_log_f'][i], p['ssd_a_log_b'][i]]).astype(F32)
    lay['ssd_par'] = jnp.pad(par, ((0, 0), (0, DT_PAD - SSD_HEADS)))
    lay['ssd_part'] = par.T
    lay['ssd_d'] = _row(jnp.repeat(p['ssd_d'][i], SSD_HEAD_DIM))
    lay['ssd_norm_g'] = _row(p['ssd_norm_g'][i])
    lay['w_out'] = p['w_out'][i].astype(BF16)
    lay['norm_ffn_g'] = _row(p['norm_ffn_g'][i])
    lay['w_gate'] = p['w_gate'][i].astype(BF16)
    lay['w_up'] = p['w_up'][i].astype(BF16)
    lay['w_down'] = p['w_down'][i].astype(BF16)
    return lay


def _trunk(x3, layers, final_g):
    b, l, dm = x3.shape
    x = x3.reshape(b * l, dm)
    for i, lay in enumerate(layers):
        u, v, z, xbc, dt = _norm_inproj(x, lay['norm_mix_g'], lay['w_in'])
        s5f, s5b = _s5_scan(u.reshape(b, l, -1), *lay['s5'])
        fn = _fnet(v.reshape(b, l, -1), *lay['fnet_ab'], lay['fnet_b'])
        xbc_c = _ssd_conv(xbc.reshape(b, l, -1), lay['conv_w'], lay['conv_b'])
        sdf, sdb = _ssd_scan(xbc_c, dt.reshape(b, l, -1), lay['ssd_par'], lay['ssd_part'])
        n = b * l
        x = _mix_out(x, s5f.reshape(n, -1), s5b.reshape(n, -1), u, fn.reshape(n, -1),
                     sdf.reshape(n, -1), sdb.reshape(n, -1), xbc_c.reshape(n, -1), z,
                     lay['s5_d'], lay['s5_w_glu'], lay['s5_b_glu'], lay['ssd_d'], lay['ssd_norm_g'], lay['w_out'])
        x = _ffn(x, lay['norm_ffn_g'], lay['w_gate'], lay['w_up'], lay['w_down'], final_g,
                 final_norm=(i == len(layers) - 1))
    return x.reshape(b, l, dm)


def kernel(x_prompt, x_sample, norm_mix_g, w_in, s5_b_re, s5_b_im, s5_lam_re_f, s5_lam_im_f, s5_log_step_f, s5_c_re_f, s5_c_im_f, s5_lam_re_b, s5_lam_im_b, s5_log_step_b, s5_c_re_b, s5_c_im_b, s5_d, s5_w_glu, s5_b_glu, fnet_w, fnet_b, ssd_conv_w, ssd_conv_b, ssd_a_log_f, ssd_dt_bias_f, ssd_a_log_b, ssd_dt_bias_b, ssd_d, ssd_norm_g, w_out, norm_ffn_g, w_gate, w_up, w_down, final_norm_g):
    p = dict(norm_mix_g=norm_mix_g, w_in=w_in, s5_b_re=s5_b_re, s5_b_im=s5_b_im,
             s5_lam_re_f=s5_lam_re_f, s5_lam_im_f=s5_lam_im_f, s5_log_step_f=s5_log_step_f,
             s5_c_re_f=s5_c_re_f, s5_c_im_f=s5_c_im_f,
             s5_lam_re_b=s5_lam_re_b, s5_lam_im_b=s5_lam_im_b, s5_log_step_b=s5_log_step_b,
             s5_c_re_b=s5_c_re_b, s5_c_im_b=s5_c_im_b,
             s5_d=s5_d, s5_w_glu=s5_w_glu, s5_b_glu=s5_b_glu, fnet_w=fnet_w, fnet_b=fnet_b,
             ssd_conv_w=ssd_conv_w, ssd_conv_b=ssd_conv_b, ssd_a_log_f=ssd_a_log_f,
             ssd_dt_bias_f=ssd_dt_bias_f, ssd_a_log_b=ssd_a_log_b, ssd_dt_bias_b=ssd_dt_bias_b,
             ssd_d=ssd_d, ssd_norm_g=ssd_norm_g, w_out=w_out, norm_ffn_g=norm_ffn_g,
             w_gate=w_gate, w_up=w_up, w_down=w_down)
    layers = [_prepare_layer(i, p) for i in range(norm_mix_g.shape[0])]
    final_g = _row(final_norm_g)
    return (_trunk(x_prompt, layers, final_g), _trunk(x_sample, layers, final_g))
```

```python
import functools
import math

import numpy as np
import jax
import jax.numpy as jnp
from jax import lax
from jax.experimental import pallas as pl
from jax.experimental.pallas import tpu as pltpu

F32 = jnp.float32
BF16 = jnp.bfloat16

D_MODEL = 1024
S5_WIDTH = 256
S5_GROUP = 16
S5_GROUPS = 16
S5_STATE = 64
S5_NSTATE = S5_GROUPS * S5_STATE
FNET_WIDTH = 256
FNET_GROUPS = 4
FNET_GROUP = 64
SSD_WIDTH = 512
SSD_HEAD_DIM = 64
SSD_HEADS = 8
SSD_GROUPS = 2
SSD_STATE = 128
SSD_CONV = 5
SSD_BC = SSD_GROUPS * SSD_STATE
SSD_CONV_DIM = SSD_WIDTH + 2 * SSD_BC
DT_PAD = 128
IN_PROJ_PAD = S5_WIDTH + FNET_WIDTH + SSD_WIDTH + SSD_CONV_DIM + DT_PAD
D_FF = 2816
EPS = 1e-6

LANES = 128
SUBLANES = 8
VMEM_LIMIT = 56 * 1024 * 1024

TOKEN_TILE = 512
FFN_TILE = 256
S5_CHUNK = 256
S5_SEQS = 2
SSD_CHUNK = 128
SSD_SUB = 4
FNET_L2 = 128
FNET_KB = 8
FNET_COLS = 4096


def _rmsnorm(x, g):
    return x * lax.rsqrt(jnp.mean(x * x, axis=-1, keepdims=True) + EPS) * g


def _split3(v):
    hi = v.astype(BF16)
    r = v - hi.astype(F32)
    mid = r.astype(BF16)
    lo = (r - mid.astype(F32)).astype(BF16)
    return hi, mid, lo


def _dot(a, b):
    return jnp.dot(a, b, preferred_element_type=F32)


def _dot_exact_lhs(m, v):
    hi, mid, lo = _split3(v)
    return _dot(m, hi) + _dot(m, mid) + _dot(m, lo)


def _dot_exact_rhs(v, m):
    hi, mid, lo = _split3(v)
    return _dot(hi, m) + _dot(mid, m) + _dot(lo, m)


def _dot_hilo(m_hi, m_lo, v):
    v_hi = v.astype(BF16)
    v_lo = (v - v_hi.astype(F32)).astype(BF16)
    return _dot(m_hi, v_hi) + _dot(m_hi, v_lo) + _dot(m_lo, v_hi)


def _const_spec(shape):
    nd = len(shape)
    return pl.BlockSpec(shape, lambda *_: (0,) * nd, pipeline_mode=pl.Buffered(1))


def _params(semantics):
    return pltpu.CompilerParams(dimension_semantics=semantics, vmem_limit_bytes=VMEM_LIMIT)


_INPROJ_SPLITS = (0, S5_WIDTH, S5_WIDTH + FNET_WIDTH, S5_WIDTH + FNET_WIDTH + SSD_WIDTH,
                  S5_WIDTH + FNET_WIDTH + SSD_WIDTH + SSD_CONV_DIM, IN_PROJ_PAD)


def _norm_inproj_kernel(prev_ref, x_ref, next_ref, g_ref, w_ref, cw_ref, cb_ref,
                        u_ref, v_ref, z_ref, xbc_ref, dt_ref, ext, *, tiles_per_seq):
    i = pl.program_id(0)
    t_len = x_ref.shape[0]
    pad = SSD_CONV // 2
    g = g_ref[...]
    h = _rmsnorm(x_ref[...], g).astype(BF16)
    lo_xbc, hi_xbc = _INPROJ_SPLITS[3], _INPROJ_SPLITS[4]
    w_xbc = w_ref[:, lo_xbc:hi_xbc]
    first = i % tiles_per_seq == 0
    last = i % tiles_per_seq == tiles_per_seq - 1
    h_prev = _rmsnorm(prev_ref[...], g).astype(BF16)
    h_next = _rmsnorm(next_ref[...], g).astype(BF16)
    ext[0:SUBLANES] = jnp.where(first, 0.0, _dot(h_prev, w_xbc))
    ext[SUBLANES:SUBLANES + t_len] = _dot(h, w_xbc)
    ext[SUBLANES + t_len:2 * SUBLANES + t_len] = jnp.where(last, 0.0, _dot(h_next, w_xbc))
    for o_ref, lo, hi in zip((u_ref, v_ref, z_ref, None, dt_ref), _INPROJ_SPLITS[:-1], _INPROJ_SPLITS[1:]):
        if o_ref is not None:
            o_ref[...] = _dot(h, w_ref[:, lo:hi])
    rows = t_len + 2 * SUBLANES
    xe = ext[...]
    acc = jnp.broadcast_to(cb_ref[...], (t_len, SSD_CONV_DIM))
    for k in range(SSD_CONV):
        tap = xe if k == pad else pltpu.roll(xe, (pad - k) % rows, axis=0)
        acc = acc + cw_ref[k:k + 1, :] * tap[SUBLANES:SUBLANES + t_len]
    xbc_ref[...] = acc * jax.nn.sigmoid(acc)


def _norm_inproj(x, g, w, conv_w, conv_b, seq_len):
    n = x.shape[0]
    widths = [hi - lo for lo, hi in zip(_INPROJ_SPLITS[:-1], _INPROJ_SPLITS[1:])]
    per = TOKEN_TILE // SUBLANES
    last = n // SUBLANES - 1
    return pl.pallas_call(
        functools.partial(_norm_inproj_kernel, tiles_per_seq=seq_len // TOKEN_TILE),
        out_shape=[jax.ShapeDtypeStruct((n, wd), F32) for wd in widths],
        grid=(n // TOKEN_TILE,),
        in_specs=[pl.BlockSpec((SUBLANES, D_MODEL), lambda i: (jnp.maximum(i * per - 1, 0), 0)),
                  pl.BlockSpec((TOKEN_TILE, D_MODEL), lambda i: (i, 0)),
                  pl.BlockSpec((SUBLANES, D_MODEL), lambda i: (jnp.minimum((i + 1) * per, last), 0)),
                  _const_spec((1, D_MODEL)), _const_spec((D_MODEL, IN_PROJ_PAD)),
                  _const_spec(conv_w.shape), _const_spec(conv_b.shape)],
        out_specs=[pl.BlockSpec((TOKEN_TILE, wd), lambda i: (i, 0)) for wd in widths],
        scratch_shapes=[pltpu.VMEM((TOKEN_TILE + 2 * SUBLANES, SSD_CONV_DIM), F32)],
        compiler_params=_params(("parallel",)),
        name="norm_inproj",
    )(x, x, x, g, w, conv_w, conv_b)


def _s5_scan_kernel(uf_ref, ub_ref, bmat_ref, cmat_ref, lam_ref, yf_ref, yb_ref, bubuf, hbuf, carry):
    c = pl.program_id(1)
    nseq, t_len = uf_ref.shape[0], uf_ref.shape[1]
    chains = [(i, d) for i in range(nseq) for d in range(2)]

    @pl.when(c == 0)
    def _():
        carry[...] = jnp.zeros_like(carry)

    for i, d in chains:
        u16 = (uf_ref if d == 0 else ub_ref)[i].astype(BF16)
        for part in range(2):
            col = (2 * d + part) * S5_NSTATE
            bu = _dot(u16, bmat_ref[:, col:col + S5_NSTATE])
            k = (i * 2 + d) * 2 + part
            for j in range(SUBLANES):
                bubuf[k, pl.ds(j, t_len, stride=SUBLANES), :] = bu[:, j * LANES:(j + 1) * LANES]

    lam = [lam_ref[k] for k in range(4)]

    def step(t, hs):
        out = []
        for n, (i, d) in enumerate(chains):
            kre = (i * 2 + d) * 2
            row = pl.multiple_of((t if d == 0 else t_len - 1 - t) * SUBLANES, SUBLANES)
            a_re, a_im = lam[2 * d], lam[2 * d + 1]
            h_re, h_im = hs[2 * n], hs[2 * n + 1]
            n_re = a_re * h_re - a_im * h_im + bubuf[kre, pl.ds(row, SUBLANES), :]
            n_im = a_re * h_im + a_im * h_re + bubuf[kre + 1, pl.ds(row, SUBLANES), :]
            hbuf[kre, pl.ds(row, SUBLANES), :] = n_re
            hbuf[kre + 1, pl.ds(row, SUBLANES), :] = n_im
            out += [n_re, n_im]
        return tuple(out)

    hs = lax.fori_loop(0, t_len, step, tuple(carry[k] for k in range(4 * nseq)), unroll=8)
    for k in range(4 * nseq):
        carry[k] = hs[k]

    for i, d in chains:
        kre = (i * 2 + d) * 2
        h = jnp.concatenate(
            [hbuf[kre + part, pl.ds(j, t_len, stride=SUBLANES), :] for part in range(2) for j in range(SUBLANES)],
            axis=1).astype(BF16)
        (yf_ref if d == 0 else yb_ref)[i] = _dot(h, cmat_ref[d])


def _s5_scan(u, bmat, cmat, lam):
    b, l, _ = u.shape
    nc = l // S5_CHUNK
    blk = (S5_SEQS, S5_CHUNK, S5_WIDTH)
    fwd = pl.BlockSpec(blk, lambda i, c: (i, c, 0))
    bwd = pl.BlockSpec(blk, lambda i, c: (i, nc - 1 - c, 0))
    return pl.pallas_call(
        _s5_scan_kernel,
        out_shape=[jax.ShapeDtypeStruct(u.shape, F32)] * 2,
        grid=(b // S5_SEQS, nc),
        in_specs=[fwd, bwd, _const_spec(bmat.shape), _const_spec(cmat.shape), _const_spec(lam.shape)],
        out_specs=[fwd, bwd],
        scratch_shapes=[pltpu.VMEM((S5_SEQS * 4, S5_CHUNK * SUBLANES, LANES), F32),
                        pltpu.VMEM((S5_SEQS * 4, S5_CHUNK * SUBLANES, LANES), F32),
                        pltpu.VMEM((S5_SEQS * 4, SUBLANES, LANES), F32)],
        compiler_params=_params(("parallel", "arbitrary")),
        name="s5_scan",
    )(u, u, bmat, cmat, lam)


def _s5_prepare(b_re, b_im, dirs):
    g, p, c = S5_GROUPS, S5_STATE, S5_GROUP
    eye = jnp.eye(g, dtype=F32)
    bcols, cmats, lams = [], [], []
    for lam_re, lam_im, log_step, c_re, c_im in dirs:
        step = jnp.exp(log_step)[:, None]
        mag = jnp.exp(lam_re * step)
        lb_re, lb_im = mag * jnp.cos(lam_im * step), mag * jnp.sin(lam_im * step)
        den = lam_re * lam_re + lam_im * lam_im
        q_re = ((lb_re - 1.0) * lam_re + lb_im * lam_im) / den
        q_im = (lb_im * lam_re - (lb_re - 1.0) * lam_im) / den
        bb_re = q_re[..., None] * b_re - q_im[..., None] * b_im
        bb_im = q_re[..., None] * b_im + q_im[..., None] * b_re
        for bb in (bb_re, bb_im):
            bcols.append(jnp.einsum('gpc,gh->gchp', bb, eye).reshape(g * c, g * p))
        cm = [jnp.einsum('gcp,gh->gphc', cc, eye).reshape(g * p, g * c) for cc in (c_re, -c_im)]
        cmats.append(jnp.concatenate(cm, axis=0))
        lams += [lb_re.reshape(SUBLANES, LANES), lb_im.reshape(SUBLANES, LANES)]
    return (jnp.concatenate(bcols, axis=1).astype(BF16), jnp.stack(cmats).astype(BF16), jnp.stack(lams))


def _dft_parts(n, scale):
    k = np.arange(n)
    ang = 2.0 * np.pi * ((k[:, None] * k[None, :]) % n) / n
    return np.cos(ang) * scale, np.sin(ang) * scale


def _hilo(m):
    m = np.asarray(m, np.float32)
    hi = jnp.asarray(m, F32).astype(BF16)
    lo = (jnp.asarray(m, F32) - hi.astype(F32)).astype(BF16)
    return hi, lo


def _fnet_fold_kernel(ch_ref, cl_ref, sh_ref, sl_ref, w_ref, a_ref, b_ref):
    w = w_ref[...]
    a_ref[...] = _dot_hilo(ch_ref[...], cl_ref[...], w).astype(BF16)
    b_ref[...] = _dot_hilo(sh_ref[...], sl_ref[...], w).astype(BF16)


def _fnet_fold(w):
    c64, s64 = _dft_parts(FNET_GROUP, 1.0 / math.sqrt(FNET_GROUP))
    eye = np.eye(FNET_GROUPS)
    cbd, sbd = np.kron(eye, c64), np.kron(eye, s64)
    wbd = jnp.einsum('gcd,gh->gchd', w, jnp.eye(FNET_GROUPS, dtype=F32)).reshape(FNET_WIDTH, FNET_WIDTH)
    shape = (FNET_WIDTH, FNET_WIDTH)
    return pl.pallas_call(
        _fnet_fold_kernel,
        out_shape=[jax.ShapeDtypeStruct(shape, BF16)] * 2,
        name="fnet_fold",
    )(*_hilo(cbd), *_hilo(sbd), wbd)


def _fnet_stage1_kernel(x_ref, fh_ref, fl_ref, gre_ref, gim_ref):
    l1 = x_ref.shape[1]
    g = _dot_hilo(fh_ref[...], fl_ref[...], x_ref[0])
    gre_ref[0] = g[:l1]
    gim_ref[0] = g[l1:]


def _fnet_stage2_kernel(gre_ref, gim_ref, twr_ref, twi_ref, fh_ref, fl_ref, a_ref, b_ref, bias_ref, o_ref):
    l2 = gre_ref.shape[2]
    for i in range(gre_ref.shape[1]):
        gr, gi = gre_ref[0, i], gim_ref[0, i]
        tr = jnp.concatenate([twr_ref[i]] * (FNET_WIDTH // LANES), axis=1)
        ti = jnp.concatenate([twi_ref[i]] * (FNET_WIDTH // LANES), axis=1)
        s = jnp.concatenate([gr * tr - gi * ti, gr * ti + gi * tr], axis=0)
        y = _dot_hilo(fh_ref[...], fl_ref[...], s)
        out = _dot(y[:l2].astype(BF16), a_ref[...]) + _dot(y[l2:].astype(BF16), b_ref[...]) + bias_ref[...]
        o_ref[0, :, i * FNET_WIDTH:(i + 1) * FNET_WIDTH] = out


def _fnet(v, a, bm, bias):
    b, l, c = v.shape
    l2 = FNET_L2
    l1 = l // l2
    c1, s1 = _dft_parts(l1, 1.0 / math.sqrt(l1))
    f1h, f1l = _hilo(np.concatenate([c1, -s1], axis=0))
    c2, s2 = _dft_parts(l2, 1.0 / math.sqrt(l2))
    f2h, f2l = _hilo(np.block([[c2, s2], [-s2, c2]]))
    m = (np.arange(l1)[:, None] * np.arange(l2)[None, :]) % l
    ang = 2.0 * np.pi * m / l
    twr = jnp.broadcast_to(jnp.asarray(np.cos(ang), F32)[:, :, None], (l1, l2, LANES))
    twi = jnp.broadcast_to(jnp.asarray(-np.sin(ang), F32)[:, :, None], (l1, l2, LANES))

    cols = min(FNET_COLS, l2 * c)
    xs = pl.BlockSpec((1, l1, cols), lambda i, j: (i, 0, j))
    gre, gim = pl.pallas_call(
        _fnet_stage1_kernel,
        out_shape=[jax.ShapeDtypeStruct((b, l1, l2 * c), F32)] * 2,
        grid=(b, l2 * c // cols),
        in_specs=[xs, _const_spec(f1h.shape), _const_spec(f1l.shape)],
        out_specs=[xs, xs],
        compiler_params=_params(("parallel", "parallel")),
        name="fnet_stage1",
    )(v.reshape(b, l1, l2 * c), f1h, f1l)

    kb = min(FNET_KB, l1)
    gs = pl.BlockSpec((1, kb, l2, c), lambda i, j: (i, j, 0, 0))
    ts = pl.BlockSpec((kb, l2, LANES), lambda i, j: (j, 0, 0))
    y = pl.pallas_call(
        _fnet_stage2_kernel,
        out_shape=jax.ShapeDtypeStruct((b, l2, l1 * c), F32),
        grid=(b, l1 // kb),
        in_specs=[gs, gs, ts, ts, _const_spec(f2h.shape), _const_spec(f2l.shape),
                  _const_spec(a.shape), _const_spec(bm.shape), _const_spec(bias.shape)],
        out_specs=pl.BlockSpec((1, l2, kb * c), lambda i, j: (i, 0, j)),
        compiler_params=_params(("parallel", "parallel")),
        name="fnet_stage2",
    )(gre.reshape(b, l1, l2, c), gim.reshape(b, l1, l2, c), twr, twi, f2h, f2l, a, bm, bias)
    return y.reshape(b, l, c)


def _ones_where(mask):
    return jnp.where(mask, 1.0, 0.0).astype(BF16)


def _softplus(x):
    return jnp.maximum(x, 0.0) + jnp.log1p(jnp.exp(-jnp.abs(x)))


def _ssd_dt_selector():
    sel = np.zeros((2, LANES, SSD_WIDTH), np.float32)
    for d in range(2):
        for lane in range(SSD_WIDTH):
            sel[d, 2 * d * SSD_HEADS + lane // SSD_HEAD_DIM, lane] = 1.0
    return jnp.asarray(sel, BF16)


def _ssd_scan_kernel(xf_ref, xb_ref, dtf_ref, dtb_ref, part_ref, sel_ref, yf_ref, yb_ref, state):
    c = pl.program_id(1)
    t_len = SSD_CHUNK
    n_sub = xf_ref.shape[1] // t_len
    pair_w = 2 * SSD_HEAD_DIM
    heads_per_pair = pair_w // SSD_HEAD_DIM
    n_pairs = SSD_WIDTH // pair_w
    pairs_per_group = n_pairs // SSD_GROUPS

    @pl.when(c == 0)
    def _():
        state[...] = jnp.zeros_like(state)

    row = lax.broadcasted_iota(jnp.int32, (t_len, t_len), 0)
    col = lax.broadcasted_iota(jnp.int32, (t_len, t_len), 1)
    lower, upper = _ones_where(row >= col), _ones_where(row <= col)
    first_head = lax.broadcasted_iota(jnp.int32, (1, pair_w), 1) < SSD_HEAD_DIM
    states = [[state[d, q] for q in range(n_pairs)] for d in range(2)]

    for k in range(n_sub):
        at = [k * t_len, (n_sub - 1 - k) * t_len]
        rows = []
        for d, dt_ref in enumerate((dtf_ref, dtb_ref)):
            dt_row = _softplus(dt_ref[0, at[d]:at[d] + t_len, :].T[0:SSD_HEADS] + part_ref[:, d:d + 1])
            rows += [dt_row, dt_row * -jnp.exp(part_ref[:, 2 + d:3 + d])]
        n_rows = len(rows) * SSD_HEADS
        row_form = jnp.concatenate(rows, axis=0)
        col_form = jnp.concatenate([row_form, jnp.zeros((LANES - n_rows, t_len), F32)], axis=0).T
        cum_col = _dot_exact_lhs(lower, col_form)
        cum_row = _dot_exact_rhs(row_form, upper)
        cf_hi = col_form.astype(BF16)
        cf_lo = (col_form - cf_hi.astype(F32)).astype(BF16)

        for d, (x_ref, y_ref) in enumerate(((xf_ref, yf_ref), (xb_ref, yb_ref))):
            da_at = (2 * d + 1) * SSD_HEADS
            if d == 0:
                keep = row >= col
                p_col = cum_col
                p_row = cum_row[da_at:da_at + SSD_HEADS]
            else:
                keep = col >= row
                p_col = col_form - cum_col
                p_row = row_form[da_at:da_at + SSD_HEADS] - cum_row[da_at:da_at + SSD_HEADS]

            xbc = x_ref[0, at[d]:at[d] + t_len, :]
            dt_wide = _dot(cf_hi, sel_ref[d]) + _dot(cf_lo, sel_ref[d])
            for g in range(SSD_GROUPS):
                b_g = xbc[:, SSD_WIDTH + g * SSD_STATE:SSD_WIDTH + (g + 1) * SSD_STATE]
                c_lo = SSD_WIDTH + SSD_BC + g * SSD_STATE
                c_g = xbc[:, c_lo:c_lo + SSD_STATE].astype(BF16)
                bt_g = b_g.T.astype(BF16)
                cb = _dot(c_g, bt_g)
                for q in range(g * pairs_per_group, (g + 1) * pairs_per_group):
                    h0 = q * heads_per_pair

                    def expand(m):
                        return jnp.where(first_head, m[:, da_at + h0:da_at + h0 + 1],
                                         m[:, da_at + h0 + 1:da_at + h0 + 2])

                    p_e = expand(p_col)
                    xdt = xbc[:, q * pair_w:(q + 1) * pair_w] * dt_wide[:, q * pair_w:(q + 1) * pair_w]
                    if d == 0:
                        total = p_e[t_len - 1:t_len]
                        off_scale = jnp.exp(p_e)
                        w_state = jnp.exp(total - p_e)
                    else:
                        total = expand(col_form[t_len - 1:t_len]) - p_e[t_len - 1:t_len]
                        off_scale = jnp.exp(total + p_e)
                        w_state = jnp.exp(-p_e)
                    st = states[d][q]
                    y = _dot(c_g, st.astype(BF16)) * off_scale
                    xdt16 = xdt.astype(BF16)
                    intra = []
                    for h in range(h0, h0 + heads_per_pair):
                        diff = p_col[:, da_at + h:da_at + h + 1] - p_row[h:h + 1, :]
                        m = cb * jnp.exp(jnp.where(keep, diff, -jnp.inf))
                        intra.append(_dot(m.astype(BF16), xdt16))
                    y_ref[0, at[d]:at[d] + t_len, q * pair_w:(q + 1) * pair_w] = (
                        y + jnp.where(first_head, intra[0], intra[1]))
                    states[d][q] = st * jnp.exp(total) + _dot(bt_g, (xdt * w_state).astype(BF16))

    for d in range(2):
        for q in range(n_pairs):
            state[d, q] = states[d][q]


def _ssd_scan(xbc, dt, part):
    b, l, ch = xbc.shape
    t_blk = SSD_CHUNK * SSD_SUB
    nb = l // t_blk
    pairs = SSD_WIDTH // (2 * SSD_HEAD_DIM)
    sel = _ssd_dt_selector()

    def fwd(width):
        return pl.BlockSpec((1, t_blk, width), lambda i, c: (i, c, 0))

    def bwd(width):
        return pl.BlockSpec((1, t_blk, width), lambda i, c: (i, nb - 1 - c, 0))

    return pl.pallas_call(
        _ssd_scan_kernel,
        out_shape=[jax.ShapeDtypeStruct((b, l, SSD_WIDTH), F32)] * 2,
        grid=(b, nb),
        in_specs=[fwd(ch), bwd(ch), fwd(DT_PAD), bwd(DT_PAD), _const_spec(part.shape), _const_spec(sel.shape)],
        out_specs=[fwd(SSD_WIDTH), bwd(SSD_WIDTH)],
        scratch_shapes=[pltpu.VMEM((2, pairs, SSD_STATE, 2 * SSD_HEAD_DIM), F32)],
        compiler_params=_params(("parallel", "arbitrary")),
        name="ssd_scan",
    )(xbc, xbc, dt, dt, part, sel)


def _mix_ffn_kernel(x_ref, s5f_ref, s5b_ref, u_ref, fn_ref, sdf_ref, sdb_ref, xc_ref, z_ref,
                    s5d_ref, wglu_ref, bglu_ref, sdd_ref, sdg_ref, wo_ref,
                    g_ref, wg_ref, wu_ref, wd_ref, gf_ref, o_ref, *, final_norm):
    ya = s5f_ref[...] + s5b_ref[...] + u_ref[...] * s5d_ref[...]
    ga = jax.nn.gelu(ya)
    ya = ga * jax.nn.sigmoid(_dot(ga.astype(BF16), wglu_ref[...]) + bglu_ref[...])
    z = z_ref[...]
    yc = (sdf_ref[...] + sdb_ref[...] + xc_ref[...] * sdd_ref[...]) * (z * jax.nn.sigmoid(z))
    yc = _rmsnorm(yc, sdg_ref[...])
    lo, mid = S5_WIDTH, S5_WIDTH + FNET_WIDTH
    x = (x_ref[...] + _dot(ya.astype(BF16), wo_ref[0:lo, :])
         + _dot(fn_ref[...].astype(BF16), wo_ref[lo:mid, :])
         + _dot(yc.astype(BF16), wo_ref[mid:, :]))
    h = _rmsnorm(x, g_ref[...]).astype(BF16)
    gate = _dot(h, wg_ref[...])
    act = (gate * jax.nn.sigmoid(gate) * _dot(h, wu_ref[...])).astype(BF16)
    y = x + _dot(act, wd_ref[...])
    o_ref[...] = _rmsnorm(y, gf_ref[...]) if final_norm else y


def _mix_ffn(x, s5f, s5b, u, fn, sdf, sdb, xbc_c, z, consts, final_norm):
    n = x.shape[0]

    def rows(width):
        return pl.BlockSpec((FFN_TILE, width), lambda i: (i, 0))

    return pl.pallas_call(
        functools.partial(_mix_ffn_kernel, final_norm=final_norm),
        out_shape=jax.ShapeDtypeStruct(x.shape, F32),
        grid=(n // FFN_TILE,),
        in_specs=[rows(D_MODEL), rows(S5_WIDTH), rows(S5_WIDTH), rows(S5_WIDTH), rows(FNET_WIDTH),
                  rows(SSD_WIDTH), rows(SSD_WIDTH), rows(SSD_WIDTH), rows(SSD_WIDTH)]
                 + [_const_spec(a.shape) for a in consts],
        out_specs=rows(D_MODEL),
        compiler_params=_params(("parallel",)),
        name="mix_ffn",
    )(x, s5f, s5b, u, fn, sdf, sdb, xbc_c, z, *consts)


def _row(v):
    return v.reshape(1, -1).astype(F32)


def _prepare_layer(i, p):
    lay = {}
    lay['norm_mix_g'] = _row(p['norm_mix_g'][i])
    lay['w_in'] = jnp.pad(p['w_in'][i], ((0, 0), (0, IN_PROJ_PAD - p['w_in'].shape[-1]))).astype(BF16)
    lay['s5'] = _s5_prepare(
        p['s5_b_re'][i], p['s5_b_im'][i],
        [(p['s5_lam_re_f'][i], p['s5_lam_im_f'][i], p['s5_log_step_f'][i], p['s5_c_re_f'][i], p['s5_c_im_f'][i]),
         (p['s5_lam_re_b'][i], p['s5_lam_im_b'][i], p['s5_log_step_b'][i], p['s5_c_re_b'][i], p['s5_c_im_b'][i])])
    lay['s5_d'] = _row(p['s5_d'][i])
    lay['s5_w_glu'] = p['s5_w_glu'][i].astype(BF16)
    lay['s5_b_glu'] = _row(p['s5_b_glu'][i])
    lay['fnet_ab'] = _fnet_fold(p['fnet_w'][i])
    lay['fnet_b'] = _row(p['fnet_b'][i])
    lay['conv_w'] = p['ssd_conv_w'][i].astype(F32)
    lay['conv_b'] = _row(p['ssd_conv_b'][i])
    par = jnp.stack([p['ssd_dt_bias_f'][i], p['ssd_dt_bias_b'][i], p['ssd_a_log_f'][i], p['ssd_a_log_b'][i]]).astype(F32)
    lay['ssd_part'] = par.T
    lay['ssd_d'] = _row(jnp.repeat(p['ssd_d'][i], SSD_HEAD_DIM))
    lay['ssd_norm_g'] = _row(p['ssd_norm_g'][i])
    lay['w_out'] = p['w_out'][i].astype(BF16)
    lay['norm_ffn_g'] = _row(p['norm_ffn_g'][i])
    lay['w_gate'] = p['w_gate'][i].astype(BF16)
    lay['w_up'] = p['w_up'][i].astype(BF16)
    lay['w_down'] = p['w_down'][i].astype(BF16)
    return lay


def _trunk(x3, layers, final_g):
    b, l, dm = x3.shape
    x = x3.reshape(b * l, dm)
    for i, lay in enumerate(layers):
        u, v, z, xbc_c, dt = _norm_inproj(x, lay['norm_mix_g'], lay['w_in'], lay['conv_w'], lay['conv_b'], l)
        s5f, s5b = _s5_scan(u.reshape(b, l, -1), *lay['s5'])
        fn = _fnet(v.reshape(b, l, -1), *lay['fnet_ab'], lay['fnet_b'])
        sdf, sdb = _ssd_scan(xbc_c.reshape(b, l, -1), dt.reshape(b, l, -1), lay['ssd_part'])
        n = b * l
        consts = (lay['s5_d'], lay['s5_w_glu'], lay['s5_b_glu'], lay['ssd_d'], lay['ssd_norm_g'], lay['w_out'],
                  lay['norm_ffn_g'], lay['w_gate'], lay['w_up'], lay['w_down'], final_g)
        x = _mix_ffn(x, s5f.reshape(n, -1), s5b.reshape(n, -1), u, fn.reshape(n, -1),
                     sdf.reshape(n, -1), sdb.reshape(n, -1), xbc_c, z, consts,
                     final_norm=(i == len(layers) - 1))
    return x.reshape(b, l, dm)


def kernel(x_prompt, x_sample, norm_mix_g, w_in, s5_b_re, s5_b_im, s5_lam_re_f, s5_lam_im_f, s5_log_step_f, s5_c_re_f, s5_c_im_f, s5_lam_re_b, s5_lam_im_b, s5_log_step_b, s5_c_re_b, s5_c_im_b, s5_d, s5_w_glu, s5_b_glu, fnet_w, fnet_b, ssd_conv_w, ssd_conv_b, ssd_a_log_f, ssd_dt_bias_f, ssd_a_log_b, ssd_dt_bias_b, ssd_d, ssd_norm_g, w_out, norm_ffn_g, w_gate, w_up, w_down, final_norm_g):
    p = dict(norm_mix_g=norm_mix_g, w_in=w_in, s5_b_re=s5_b_re, s5_b_im=s5_b_im,
             s5_lam_re_f=s5_lam_re_f, s5_lam_im_f=s5_lam_im_f, s5_log_step_f=s5_log_step_f,
             s5_c_re_f=s5_c_re_f, s5_c_im_f=s5_c_im_f,
             s5_lam_re_b=s5_lam_re_b, s5_lam_im_b=s5_lam_im_b, s5_log_step_b=s5_log_step_b,
             s5_c_re_b=s5_c_re_b, s5_c_im_b=s5_c_im_b,
             s5_d=s5_d, s5_w_glu=s5_w_glu, s5_b_glu=s5_b_glu, fnet_w=fnet_w, fnet_b=fnet_b,
             ssd_conv_w=ssd_conv_w, ssd_conv_b=ssd_conv_b, ssd_a_log_f=ssd_a_log_f,
             ssd_dt_bias_f=ssd_dt_bias_f, ssd_a_log_b=ssd_a_log_b, ssd_dt_bias_b=ssd_dt_bias_b,
             ssd_d=ssd_d, ssd_norm_g=ssd_norm_g, w_out=w_out, norm_ffn_g=norm_ffn_g,
             w_gate=w_gate, w_up=w_up, w_down=w_down)
    layers = [_prepare_layer(i, p) for i in range(norm_mix_g.shape[0])]
    final_g = _row(final_norm_g)
    return (_trunk(x_prompt, layers, final_g), _trunk(x_sample, layers, final_g))
```

```python
import functools
import math

import numpy as np
import jax
import jax.numpy as jnp
from jax import lax
from jax.experimental import pallas as pl
from jax.experimental.pallas import tpu as pltpu

F32 = jnp.float32
BF16 = jnp.bfloat16

D_MODEL = 1024
S5_WIDTH = 256
S5_GROUP = 16
S5_GROUPS = 16
S5_STATE = 64
S5_NSTATE = S5_GROUPS * S5_STATE
FNET_WIDTH = 256
FNET_GROUPS = 4
FNET_GROUP = 64
SSD_WIDTH = 512
SSD_HEAD_DIM = 64
SSD_HEADS = 8
SSD_GROUPS = 2
SSD_STATE = 128
SSD_CONV = 5
SSD_BC = SSD_GROUPS * SSD_STATE
SSD_CONV_DIM = SSD_WIDTH + 2 * SSD_BC
DT_PAD = 128
IN_PROJ_PAD = S5_WIDTH + FNET_WIDTH + SSD_WIDTH + SSD_CONV_DIM + DT_PAD
D_FF = 2816
EPS = 1e-6

LANES = 128
SUBLANES = 8
VMEM_LIMIT = 56 * 1024 * 1024

TOKEN_TILE = 512
FFN_TILE = 512
S5_CHUNK = 256
S5_SEQS = 2
SSD_CHUNK = 128
SSD_SUB = 4
FNET_L2 = 128
FNET_KB = 8
FNET_COLS = 4096


def _rmsnorm(x, g):
    return x * lax.rsqrt(jnp.mean(x * x, axis=-1, keepdims=True) + EPS) * g


def _split3(v):
    hi = v.astype(BF16)
    r = v - hi.astype(F32)
    mid = r.astype(BF16)
    lo = (r - mid.astype(F32)).astype(BF16)
    return hi, mid, lo


def _dot(a, b):
    return jnp.dot(a, b, preferred_element_type=F32)


def _dot_exact_lhs(m, v):
    hi, mid, lo = _split3(v)
    return _dot(m, hi) + _dot(m, mid) + _dot(m, lo)


def _dot_exact_rhs(v, m):
    hi, mid, lo = _split3(v)
    return _dot(hi, m) + _dot(mid, m) + _dot(lo, m)


def _dot_hilo(m_hi, m_lo, v):
    v_hi = v.astype(BF16)
    v_lo = (v - v_hi.astype(F32)).astype(BF16)
    return _dot(m_hi, v_hi) + _dot(m_hi, v_lo) + _dot(m_lo, v_hi)


def _const_spec(shape):
    nd = len(shape)
    return pl.BlockSpec(shape, lambda *_: (0,) * nd, pipeline_mode=pl.Buffered(1))


def _params(semantics):
    return pltpu.CompilerParams(dimension_semantics=semantics, vmem_limit_bytes=VMEM_LIMIT)


_INPROJ_SPLITS = (0, S5_WIDTH, S5_WIDTH + FNET_WIDTH, S5_WIDTH + FNET_WIDTH + SSD_WIDTH,
                  S5_WIDTH + FNET_WIDTH + SSD_WIDTH + SSD_CONV_DIM, IN_PROJ_PAD)


def _norm_inproj_kernel(prev_ref, x_ref, next_ref, g_ref, w_ref, cw_ref, cb_ref,
                        u_ref, v_ref, z_ref, xbc_ref, dt_ref, ext, *, tiles_per_seq):
    i = pl.program_id(0)
    t_len = x_ref.shape[0]
    pad = SSD_CONV // 2
    g = g_ref[...]
    h = _rmsnorm(x_ref[...], g).astype(BF16)
    lo_xbc, hi_xbc = _INPROJ_SPLITS[3], _INPROJ_SPLITS[4]
    w_xbc = w_ref[:, lo_xbc:hi_xbc]
    first = i % tiles_per_seq == 0
    last = i % tiles_per_seq == tiles_per_seq - 1
    h_prev = _rmsnorm(prev_ref[...], g).astype(BF16)
    h_next = _rmsnorm(next_ref[...], g).astype(BF16)
    ext[0:SUBLANES] = jnp.where(first, 0.0, _dot(h_prev, w_xbc))
    ext[SUBLANES:SUBLANES + t_len] = _dot(h, w_xbc)
    ext[SUBLANES + t_len:2 * SUBLANES + t_len] = jnp.where(last, 0.0, _dot(h_next, w_xbc))
    for o_ref, lo, hi in zip((u_ref, v_ref, z_ref, None, dt_ref), _INPROJ_SPLITS[:-1], _INPROJ_SPLITS[1:]):
        if o_ref is not None:
            o_ref[...] = _dot(h, w_ref[:, lo:hi])
    rows = t_len + 2 * SUBLANES
    xe = ext[...]
    acc = jnp.broadcast_to(cb_ref[...], (t_len, SSD_CONV_DIM))
    for k in range(SSD_CONV):
        tap = xe if k == pad else pltpu.roll(xe, (pad - k) % rows, axis=0)
        acc = acc + cw_ref[k:k + 1, :] * tap[SUBLANES:SUBLANES + t_len]
    xbc_ref[...] = acc * jax.nn.sigmoid(acc)


def _norm_inproj(x, g, w, conv_w, conv_b, seq_len):
    n = x.shape[0]
    widths = [hi - lo for lo, hi in zip(_INPROJ_SPLITS[:-1], _INPROJ_SPLITS[1:])]
    per = TOKEN_TILE // SUBLANES
    last = n // SUBLANES - 1
    return pl.pallas_call(
        functools.partial(_norm_inproj_kernel, tiles_per_seq=seq_len // TOKEN_TILE),
        out_shape=[jax.ShapeDtypeStruct((n, wd), F32) for wd in widths],
        grid=(n // TOKEN_TILE,),
        in_specs=[pl.BlockSpec((SUBLANES, D_MODEL), lambda i: (jnp.maximum(i * per - 1, 0), 0)),
                  pl.BlockSpec((TOKEN_TILE, D_MODEL), lambda i: (i, 0)),
                  pl.BlockSpec((SUBLANES, D_MODEL), lambda i: (jnp.minimum((i + 1) * per, last), 0)),
                  _const_spec((1, D_MODEL)), _const_spec((D_MODEL, IN_PROJ_PAD)),
                  _const_spec(conv_w.shape), _const_spec(conv_b.shape)],
        out_specs=[pl.BlockSpec((TOKEN_TILE, wd), lambda i: (i, 0)) for wd in widths],
        scratch_shapes=[pltpu.VMEM((TOKEN_TILE + 2 * SUBLANES, SSD_CONV_DIM), F32)],
        compiler_params=_params(("parallel",)),
        name="norm_inproj",
    )(x, x, x, g, w, conv_w, conv_b)


def _s5_scan_kernel(uf_ref, ub_ref, bmat_ref, cmat_ref, lam_ref, yf_ref, yb_ref, bu0, bu1, h0, h1, carry):
    s = pl.program_id(1)
    nseq, t_len = uf_ref.shape[0], uf_ref.shape[1] // 2
    chains = [(i, d) for i in range(nseq) for d in range(2)]
    lam = [lam_ref[k] for k in range(4)]

    @pl.when(s == 0)
    def _():
        carry[...] = jnp.zeros_like(carry)
        bu1[...] = jnp.zeros_like(bu1)
        h0[...] = jnp.zeros_like(h0)

    def rows_of(half, d):
        lo = half * t_len if d == 0 else (1 - half) * t_len
        return pl.ds(lo, t_len)

    def project(half, bu, i, d):
        u16 = (uf_ref if d == 0 else ub_ref)[i, rows_of(half, d), :].astype(BF16)
        for part in range(2):
            col = (2 * d + part) * S5_NSTATE
            prod = _dot(u16, bmat_ref[:, col:col + S5_NSTATE])
            k = (i * 2 + d) * 2 + part
            for j in range(SUBLANES):
                bu[k, pl.ds(j, t_len, stride=SUBLANES), :] = prod[:, j * LANES:(j + 1) * LANES]

    def scan(bu, hb, hs, t_lo, t_hi):
        for t in range(t_lo, t_hi):
            for n, (i, d) in enumerate(chains):
                kre = (i * 2 + d) * 2
                row = pl.ds((t if d == 0 else t_len - 1 - t) * SUBLANES, SUBLANES)
                a_re, a_im = lam[2 * d], lam[2 * d + 1]
                h_re, h_im = hs[2 * n], hs[2 * n + 1]
                hs[2 * n] = a_re * h_re - a_im * h_im + bu[kre, row, :]
                hs[2 * n + 1] = a_re * h_im + a_im * h_re + bu[kre + 1, row, :]
                hb[kre, row, :] = hs[2 * n]
                hb[kre + 1, row, :] = hs[2 * n + 1]

    def read_out(half, hb, i, d):
        kre = (i * 2 + d) * 2
        h = jnp.concatenate(
            [hb[kre + part, pl.ds(j, t_len, stride=SUBLANES), :] for part in range(2) for j in range(SUBLANES)],
            axis=1).astype(BF16)
        (yf_ref if d == 0 else yb_ref)[i, rows_of(half, d), :] = _dot(h, cmat_ref[d])

    hs = [carry[k] for k in range(4 * nseq)]
    per = t_len // len(chains)
    for half, (bu_w, bu_r, h_w, h_r) in enumerate(((bu0, bu1, h1, h0), (bu1, bu0, h0, h1))):
        for n, (i, d) in enumerate(chains):
            project(half, bu_w, i, d)
            scan(bu_r, h_w, hs, n * per, (n + 1) * per)
            read_out(half, h_r, i, d)
    for k in range(4 * nseq):
        carry[k] = hs[k]


def _s5_scan(u, bmat, cmat, lam):
    b, l, _ = u.shape
    nb = l // (2 * S5_CHUNK)
    blk = (S5_SEQS, 2 * S5_CHUNK, S5_WIDTH)
    in_f = pl.BlockSpec(blk, lambda i, s: (i, jnp.minimum(s, nb - 1), 0))
    in_b = pl.BlockSpec(blk, lambda i, s: (i, nb - 1 - jnp.minimum(s, nb - 1), 0))
    out_f = pl.BlockSpec(blk, lambda i, s: (i, jnp.maximum(s - 1, 0), 0))
    out_b = pl.BlockSpec(blk, lambda i, s: (i, nb - 1 - jnp.maximum(s - 1, 0), 0))
    buf = pltpu.VMEM((S5_SEQS * 4, S5_CHUNK * SUBLANES, LANES), F32)
    return pl.pallas_call(
        _s5_scan_kernel,
        out_shape=[jax.ShapeDtypeStruct(u.shape, F32)] * 2,
        grid=(b // S5_SEQS, nb + 1),
        in_specs=[in_f, in_b, _const_spec(bmat.shape), _const_spec(cmat.shape), _const_spec(lam.shape)],
        out_specs=[out_f, out_b],
        scratch_shapes=[buf, buf, buf, buf, pltpu.VMEM((S5_SEQS * 4, SUBLANES, LANES), F32)],
        compiler_params=_params(("parallel", "arbitrary")),
        name="s5_scan",
    )(u, u, bmat, cmat, lam)


def _s5_prepare(b_re, b_im, dirs):
    g, p, c = S5_GROUPS, S5_STATE, S5_GROUP
    eye = jnp.eye(g, dtype=F32)
    bcols, cmats, lams = [], [], []
    for lam_re, lam_im, log_step, c_re, c_im in dirs:
        step = jnp.exp(log_step)[:, None]
        mag = jnp.exp(lam_re * step)
        lb_re, lb_im = mag * jnp.cos(lam_im * step), mag * jnp.sin(lam_im * step)
        den = lam_re * lam_re + lam_im * lam_im
        q_re = ((lb_re - 1.0) * lam_re + lb_im * lam_im) / den
        q_im = (lb_im * lam_re - (lb_re - 1.0) * lam_im) / den
        bb_re = q_re[..., None] * b_re - q_im[..., None] * b_im
        bb_im = q_re[..., None] * b_im + q_im[..., None] * b_re
        for bb in (bb_re, bb_im):
            bcols.append(jnp.einsum('gpc,gh->gchp', bb, eye).reshape(g * c, g * p))
        cm = [jnp.einsum('gcp,gh->gphc', cc, eye).reshape(g * p, g * c) for cc in (c_re, -c_im)]
        cmats.append(jnp.concatenate(cm, axis=0))
        lams += [lb_re.reshape(SUBLANES, LANES), lb_im.reshape(SUBLANES, LANES)]
    return (jnp.concatenate(bcols, axis=1).astype(BF16), jnp.stack(cmats).astype(BF16), jnp.stack(lams))


def _dft_parts(n, scale):
    k = np.arange(n)
    ang = 2.0 * np.pi * ((k[:, None] * k[None, :]) % n) / n
    return np.cos(ang) * scale, np.sin(ang) * scale


def _hilo(m):
    m = np.asarray(m, np.float32)
    hi = jnp.asarray(m, F32).astype(BF16)
    lo = (jnp.asarray(m, F32) - hi.astype(F32)).astype(BF16)
    return hi, lo


def _fnet_fold_kernel(ch_ref, cl_ref, sh_ref, sl_ref, w_ref, a_ref, b_ref):
    w = w_ref[...]
    a_ref[...] = _dot_hilo(ch_ref[...], cl_ref[...], w).astype(BF16)
    b_ref[...] = _dot_hilo(sh_ref[...], sl_ref[...], w).astype(BF16)


def _fnet_fold(w):
    c64, s64 = _dft_parts(FNET_GROUP, 1.0 / math.sqrt(FNET_GROUP))
    eye = np.eye(FNET_GROUPS)
    cbd, sbd = np.kron(eye, c64), np.kron(eye, s64)
    wbd = jnp.einsum('gcd,gh->gchd', w, jnp.eye(FNET_GROUPS, dtype=F32)).reshape(FNET_WIDTH, FNET_WIDTH)
    shape = (FNET_WIDTH, FNET_WIDTH)
    return pl.pallas_call(
        _fnet_fold_kernel,
        out_shape=[jax.ShapeDtypeStruct(shape, BF16)] * 2,
        name="fnet_fold",
    )(*_hilo(cbd), *_hilo(sbd), wbd)


def _fnet_stage1_kernel(x_ref, fh_ref, fl_ref, gre_ref, gim_ref):
    l1 = x_ref.shape[1]
    g = _dot_hilo(fh_ref[...], fl_ref[...], x_ref[0])
    gre_ref[0] = g[:l1]
    gim_ref[0] = g[l1:]


def _fnet_stage2_kernel(gre_ref, gim_ref, twr_ref, twi_ref, fh_ref, fl_ref, a_ref, b_ref, bias_ref, o_ref):
    kb, l2 = gre_ref.shape[1], gre_ref.shape[2]
    cols = []
    for i in range(kb):
        gr, gi = gre_ref[0, i], gim_ref[0, i]
        tr = jnp.concatenate([twr_ref[i]] * (FNET_WIDTH // LANES), axis=1)
        ti = jnp.concatenate([twi_ref[i]] * (FNET_WIDTH // LANES), axis=1)
        cols.append(jnp.concatenate([gr * tr - gi * ti, gr * ti + gi * tr], axis=0))
    y = _dot_hilo(fh_ref[...], fl_ref[...], jnp.concatenate(cols, axis=1))
    y_re = jnp.concatenate([y[:l2, i * FNET_WIDTH:(i + 1) * FNET_WIDTH] for i in range(kb)], axis=0)
    y_im = jnp.concatenate([y[l2:, i * FNET_WIDTH:(i + 1) * FNET_WIDTH] for i in range(kb)], axis=0)
    out = _dot(y_re.astype(BF16), a_ref[...]) + _dot(y_im.astype(BF16), b_ref[...]) + bias_ref[...]
    for i in range(kb):
        o_ref[0, :, i * FNET_WIDTH:(i + 1) * FNET_WIDTH] = out[i * l2:(i + 1) * l2]


def _fnet(v, a, bm, bias):
    b, l, c = v.shape
    l2 = FNET_L2
    l1 = l // l2
    c1, s1 = _dft_parts(l1, 1.0 / math.sqrt(l1))
    f1h, f1l = _hilo(np.concatenate([c1, -s1], axis=0))
    c2, s2 = _dft_parts(l2, 1.0 / math.sqrt(l2))
    f2h, f2l = _hilo(np.block([[c2, s2], [-s2, c2]]))
    m = (np.arange(l1)[:, None] * np.arange(l2)[None, :]) % l
    ang = 2.0 * np.pi * m / l
    twr = jnp.broadcast_to(jnp.asarray(np.cos(ang), F32)[:, :, None], (l1, l2, LANES))
    twi = jnp.broadcast_to(jnp.asarray(-np.sin(ang), F32)[:, :, None], (l1, l2, LANES))

    cols = min(FNET_COLS, l2 * c)
    xs = pl.BlockSpec((1, l1, cols), lambda i, j: (i, 0, j))
    gre, gim = pl.pallas_call(
        _fnet_stage1_kernel,
        out_shape=[jax.ShapeDtypeStruct((b, l1, l2 * c), F32)] * 2,
        grid=(b, l2 * c // cols),
        in_specs=[xs, _const_spec(f1h.shape), _const_spec(f1l.shape)],
        out_specs=[xs, xs],
        compiler_params=_params(("parallel", "parallel")),
        name="fnet_stage1",
    )(v.reshape(b, l1, l2 * c), f1h, f1l)

    kb = min(FNET_KB, l1)
    gs = pl.BlockSpec((1, kb, l2, c), lambda i, j: (i, j, 0, 0))
    ts = pl.BlockSpec((kb, l2, LANES), lambda i, j: (j, 0, 0))
    y = pl.pallas_call(
        _fnet_stage2_kernel,
        out_shape=jax.ShapeDtypeStruct((b, l2, l1 * c), F32),
        grid=(b, l1 // kb),
        in_specs=[gs, gs, ts, ts, _const_spec(f2h.shape), _const_spec(f2l.shape),
                  _const_spec(a.shape), _const_spec(bm.shape), _const_spec(bias.shape)],
        out_specs=pl.BlockSpec((1, l2, kb * c), lambda i, j: (i, 0, j)),
        compiler_params=_params(("parallel", "parallel")),
        name="fnet_stage2",
    )(gre.reshape(b, l1, l2, c), gim.reshape(b, l1, l2, c), twr, twi, f2h, f2l, a, bm, bias)
    return y.reshape(b, l, c)


def _ones_where(mask):
    return jnp.where(mask, 1.0, 0.0).astype(BF16)


def _softplus(x):
    return jnp.maximum(x, 0.0) + jnp.log1p(jnp.exp(-jnp.abs(x)))


def _ssd_dt_selector():
    sel = np.zeros((2, LANES, SSD_WIDTH), np.float32)
    for d in range(2):
        for lane in range(SSD_WIDTH):
            sel[d, 2 * d * SSD_HEADS + lane // SSD_HEAD_DIM, lane] = 1.0
    return jnp.asarray(sel, BF16)


def _ssd_scan_kernel(xf_ref, xb_ref, dtf_ref, dtb_ref, part_ref, sel_ref, yf_ref, yb_ref, state):
    c = pl.program_id(1)
    t_len = SSD_CHUNK
    n_sub = xf_ref.shape[1] // t_len
    pair_w = 2 * SSD_HEAD_DIM
    heads_per_pair = pair_w // SSD_HEAD_DIM
    n_pairs = SSD_WIDTH // pair_w
    pairs_per_group = n_pairs // SSD_GROUPS

    @pl.when(c == 0)
    def _():
        state[...] = jnp.zeros_like(state)

    row = lax.broadcasted_iota(jnp.int32, (t_len, t_len), 0)
    col = lax.broadcasted_iota(jnp.int32, (t_len, t_len), 1)
    lower, upper = _ones_where(row >= col), _ones_where(row <= col)
    first_head = lax.broadcasted_iota(jnp.int32, (1, pair_w), 1) < SSD_HEAD_DIM
    states = [[state[d, q] for q in range(n_pairs)] for d in range(2)]

    for k in range(n_sub):
        at = [k * t_len, (n_sub - 1 - k) * t_len]
        rows = []
        for d, dt_ref in enumerate((dtf_ref, dtb_ref)):
            dt_row = _softplus(dt_ref[0, at[d]:at[d] + t_len, :].T[0:SSD_HEADS] + part_ref[:, d:d + 1])
            rows += [dt_row, dt_row * -jnp.exp(part_ref[:, 2 + d:3 + d])]
        n_rows = len(rows) * SSD_HEADS
        row_form = jnp.concatenate(rows, axis=0)
        col_form = jnp.concatenate([row_form, jnp.zeros((LANES - n_rows, t_len), F32)], axis=0).T
        cum_col = _dot_exact_lhs(lower, col_form)
        cum_row = _dot_exact_rhs(row_form, upper)
        cf_hi = col_form.astype(BF16)
        cf_lo = (col_form - cf_hi.astype(F32)).astype(BF16)

        for d, (x_ref, y_ref) in enumerate(((xf_ref, yf_ref), (xb_ref, yb_ref))):
            da_at = (2 * d + 1) * SSD_HEADS
            if d == 0:
                keep = row >= col
                p_col = cum_col
                p_row = cum_row[da_at:da_at + SSD_HEADS]
            else:
                keep = col >= row
                p_col = col_form - cum_col
                p_row = row_form[da_at:da_at + SSD_HEADS] - cum_row[da_at:da_at + SSD_HEADS]

            xbc = x_ref[0, at[d]:at[d] + t_len, :]
            dt_wide = _dot(cf_hi, sel_ref[d]) + _dot(cf_lo, sel_ref[d])
            for g in range(SSD_GROUPS):
                b_g = xbc[:, SSD_WIDTH + g * SSD_STATE:SSD_WIDTH + (g + 1) * SSD_STATE]
                c_lo = SSD_WIDTH + SSD_BC + g * SSD_STATE
                c_g = xbc[:, c_lo:c_lo + SSD_STATE].astype(BF16)
                bt_g = b_g.T.astype(BF16)
                cb = _dot(c_g, bt_g)
                for q in range(g * pairs_per_group, (g + 1) * pairs_per_group):
                    h0 = q * heads_per_pair

                    def expand(m):
                        return jnp.where(first_head, m[:, da_at + h0:da_at + h0 + 1],
                                         m[:, da_at + h0 + 1:da_at + h0 + 2])

                    p_e = expand(p_col)
                    xdt = xbc[:, q * pair_w:(q + 1) * pair_w] * dt_wide[:, q * pair_w:(q + 1) * pair_w]
                    if d == 0:
                        total = p_e[t_len - 1:t_len]
                        off_scale = jnp.exp(p_e)
                        w_state = jnp.exp(total - p_e)
                    else:
                        total = expand(col_form[t_len - 1:t_len]) - p_e[t_len - 1:t_len]
                        off_scale = jnp.exp(total + p_e)
                        w_state = jnp.exp(-p_e)
                    st = states[d][q]
                    y = _dot(c_g, st.astype(BF16)) * off_scale
                    xdt16 = xdt.astype(BF16)
                    intra = []
                    for h in range(h0, h0 + heads_per_pair):
                        diff = p_col[:, da_at + h:da_at + h + 1] - p_row[h:h + 1, :]
                        m = cb * jnp.exp(jnp.where(keep, diff, -jnp.inf))
                        intra.append(_dot(m.astype(BF16), xdt16))
                    y_ref[0, at[d]:at[d] + t_len, q * pair_w:(q + 1) * pair_w] = (
                        y + jnp.where(first_head, intra[0], intra[1]))
                    states[d][q] = st * jnp.exp(total) + _dot(bt_g, (xdt * w_state).astype(BF16))

    for d in range(2):
        for q in range(n_pairs):
            state[d, q] = states[d][q]


def _ssd_scan(xbc, dt, part):
    b, l, ch = xbc.shape
    t_blk = SSD_CHUNK * SSD_SUB
    nb = l // t_blk
    pairs = SSD_WIDTH // (2 * SSD_HEAD_DIM)
    sel = _ssd_dt_selector()

    def fwd(width):
        return pl.BlockSpec((1, t_blk, width), lambda i, c: (i, c, 0))

    def bwd(width):
        return pl.BlockSpec((1, t_blk, width), lambda i, c: (i, nb - 1 - c, 0))

    return pl.pallas_call(
        _ssd_scan_kernel,
        out_shape=[jax.ShapeDtypeStruct((b, l, SSD_WIDTH), F32)] * 2,
        grid=(b, nb),
        in_specs=[fwd(ch), bwd(ch), fwd(DT_PAD), bwd(DT_PAD), _const_spec(part.shape), _const_spec(sel.shape)],
        out_specs=[fwd(SSD_WIDTH), bwd(SSD_WIDTH)],
        scratch_shapes=[pltpu.VMEM((2, pairs, SSD_STATE, 2 * SSD_HEAD_DIM), F32)],
        compiler_params=_params(("parallel", "arbitrary")),
        name="ssd_scan",
    )(xbc, xbc, dt, dt, part, sel)


def _mix_ffn_kernel(x_ref, s5f_ref, s5b_ref, u_ref, fn_ref, sdf_ref, sdb_ref, xc_ref, z_ref,
                    s5d_ref, wglu_ref, bglu_ref, sdd_ref, sdg_ref, wo_ref,
                    g_ref, wg_ref, wu_ref, wd_ref, gf_ref, o_ref, *, final_norm):
    ya = s5f_ref[...] + s5b_ref[...] + u_ref[...] * s5d_ref[...]
    ga = jax.nn.gelu(ya)
    ya = ga * jax.nn.sigmoid(_dot(ga.astype(BF16), wglu_ref[...]) + bglu_ref[...])
    z = z_ref[...]
    yc = (sdf_ref[...] + sdb_ref[...] + xc_ref[...] * sdd_ref[...]) * (z * jax.nn.sigmoid(z))
    yc = _rmsnorm(yc, sdg_ref[...])
    lo, mid = S5_WIDTH, S5_WIDTH + FNET_WIDTH
    x = (x_ref[...] + _dot(ya.astype(BF16), wo_ref[0:lo, :])
         + _dot(fn_ref[...].astype(BF16), wo_ref[lo:mid, :])
         + _dot(yc.astype(BF16), wo_ref[mid:, :]))
    h = _rmsnorm(x, g_ref[...]).astype(BF16)
    gate = _dot(h, wg_ref[...])
    act = (gate * jax.nn.sigmoid(gate) * _dot(h, wu_ref[...])).astype(BF16)
    y = x + _dot(act, wd_ref[...])
    o_ref[...] = _rmsnorm(y, gf_ref[...]) if final_norm else y


def _mix_ffn(x, s5f, s5b, u, fn, sdf, sdb, xbc_c, z, consts, final_norm):
    n = x.shape[0]

    def rows(width):
        return pl.BlockSpec((FFN_TILE, width), lambda i: (i, 0))

    return pl.pallas_call(
        functools.partial(_mix_ffn_kernel, final_norm=final_norm),
        out_shape=jax.ShapeDtypeStruct(x.shape, F32),
        grid=(n // FFN_TILE,),
        in_specs=[rows(D_MODEL), rows(S5_WIDTH), rows(S5_WIDTH), rows(S5_WIDTH), rows(FNET_WIDTH),
                  rows(SSD_WIDTH), rows(SSD_WIDTH), rows(SSD_WIDTH), rows(SSD_WIDTH)]
                 + [_const_spec(a.shape) for a in consts],
        out_specs=rows(D_MODEL),
        compiler_params=_params(("parallel",)),
        name="mix_ffn",
    )(x, s5f, s5b, u, fn, sdf, sdb, xbc_c, z, *consts)


def _row(v):
    return v.reshape(1, -1).astype(F32)


def _prepare_layer(i, p):
    lay = {}
    lay['norm_mix_g'] = _row(p['norm_mix_g'][i])
    lay['w_in'] = jnp.pad(p['w_in'][i], ((0, 0), (0, IN_PROJ_PAD - p['w_in'].shape[-1]))).astype(BF16)
    lay['s5'] = _s5_prepare(
        p['s5_b_re'][i], p['s5_b_im'][i],
        [(p['s5_lam_re_f'][i], p['s5_lam_im_f'][i], p['s5_log_step_f'][i], p['s5_c_re_f'][i], p['s5_c_im_f'][i]),
         (p['s5_lam_re_b'][i], p['s5_lam_im_b'][i], p['s5_log_step_b'][i], p['s5_c_re_b'][i], p['s5_c_im_b'][i])])
    lay['s5_d'] = _row(p['s5_d'][i])
    lay['s5_w_glu'] = p['s5_w_glu'][i].astype(BF16)
    lay['s5_b_glu'] = _row(p['s5_b_glu'][i])
    lay['fnet_ab'] = _fnet_fold(p['fnet_w'][i])
    lay['fnet_b'] = _row(p['fnet_b'][i])
    lay['conv_w'] = p['ssd_conv_w'][i].astype(F32)
    lay['conv_b'] = _row(p['ssd_conv_b'][i])
    par = jnp.stack([p['ssd_dt_bias_f'][i], p['ssd_dt_bias_b'][i], p['ssd_a_log_f'][i], p['ssd_a_log_b'][i]]).astype(F32)
    lay['ssd_part'] = par.T
    lay['ssd_d'] = _row(jnp.repeat(p['ssd_d'][i], SSD_HEAD_DIM))
    lay['ssd_norm_g'] = _row(p['ssd_norm_g'][i])
    lay['w_out'] = p['w_out'][i].astype(BF16)
    lay['norm_ffn_g'] = _row(p['norm_ffn_g'][i])
    lay['w_gate'] = p['w_gate'][i].astype(BF16)
    lay['w_up'] = p['w_up'][i].astype(BF16)
    lay['w_down'] = p['w_down'][i].astype(BF16)
    return lay


def _trunk(x3, layers, final_g):
    b, l, dm = x3.shape
    x = x3.reshape(b * l, dm)
    for i, lay in enumerate(layers):
        u, v, z, xbc_c, dt = _norm_inproj(x, lay['norm_mix_g'], lay['w_in'], lay['conv_w'], lay['conv_b'], l)
        s5f, s5b = _s5_scan(u.reshape(b, l, -1), *lay['s5'])
        fn = _fnet(v.reshape(b, l, -1), *lay['fnet_ab'], lay['fnet_b'])
        sdf, sdb = _ssd_scan(xbc_c.reshape(b, l, -1), dt.reshape(b, l, -1), lay['ssd_part'])
        n = b * l
        consts = (lay['s5_d'], lay['s5_w_glu'], lay['s5_b_glu'], lay['ssd_d'], lay['ssd_norm_g'], lay['w_out'],
                  lay['norm_ffn_g'], lay['w_gate'], lay['w_up'], lay['w_down'], final_g)
        x = _mix_ffn(x, s5f.reshape(n, -1), s5b.reshape(n, -1), u, fn.reshape(n, -1),
                     sdf.reshape(n, -1), sdb.reshape(n, -1), xbc_c, z, consts,
                     final_norm=(i == len(layers) - 1))
    return x.reshape(b, l, dm)


def kernel(x_prompt, x_sample, norm_mix_g, w_in, s5_b_re, s5_b_im, s5_lam_re_f, s5_lam_im_f, s5_log_step_f, s5_c_re_f, s5_c_im_f, s5_lam_re_b, s5_lam_im_b, s5_log_step_b, s5_c_re_b, s5_c_im_b, s5_d, s5_w_glu, s5_b_glu, fnet_w, fnet_b, ssd_conv_w, ssd_conv_b, ssd_a_log_f, ssd_dt_bias_f, ssd_a_log_b, ssd_dt_bias_b, ssd_d, ssd_norm_g, w_out, norm_ffn_g, w_gate, w_up, w_down, final_norm_g):
    p = dict(norm_mix_g=norm_mix_g, w_in=w_in, s5_b_re=s5_b_re, s5_b_im=s5_b_im,
             s5_lam_re_f=s5_lam_re_f, s5_lam_im_f=s5_lam_im_f, s5_log_step_f=s5_log_step_f,
             s5_c_re_f=s5_c_re_f, s5_c_im_f=s5_c_im_f,
             s5_lam_re_b=s5_lam_re_b, s5_lam_im_b=s5_lam_im_b, s5_log_step_b=s5_log_step_b,
             s5_c_re_b=s5_c_re_b, s5_c_im_b=s5_c_im_b,
             s5_d=s5_d, s5_w_glu=s5_w_glu, s5_b_glu=s5_b_glu, fnet_w=fnet_w, fnet_b=fnet_b,
             ssd_conv_w=ssd_conv_w, ssd_conv_b=ssd_conv_b, ssd_a_log_f=ssd_a_log_f,
             ssd_dt_bias_f=ssd_dt_bias_f, ssd_a_log_b=ssd_a_log_b, ssd_dt_bias_b=ssd_dt_bias_b,
             ssd_d=ssd_d, ssd_norm_g=ssd_norm_g, w_out=w_out, norm_ffn_g=norm_ffn_g,
             w_gate=w_gate, w_up=w_up, w_down=w_down)
    layers = [_prepare_layer(i, p) for i in range(norm_mix_g.shape[0])]
    final_g = _row(final_norm_g)
    return (_trunk(x_prompt, layers, final_g), _trunk(x_sample, layers, final_g))
```

```python
import functools
import math

import numpy as np
import jax
import jax.numpy as jnp
from jax import lax
from jax.experimental import pallas as pl
from jax.experimental.pallas import tpu as pltpu

F32 = jnp.float32
BF16 = jnp.bfloat16

D_MODEL = 1024
S5_WIDTH = 256
S5_GROUP = 16
S5_GROUPS = 16
S5_STATE = 64
S5_NSTATE = S5_GROUPS * S5_STATE
FNET_WIDTH = 256
FNET_GROUPS = 4
FNET_GROUP = 64
SSD_WIDTH = 512
SSD_HEAD_DIM = 64
SSD_HEADS = 8
SSD_GROUPS = 2
SSD_STATE = 128
SSD_CONV = 5
SSD_BC = SSD_GROUPS * SSD_STATE
SSD_CONV_DIM = SSD_WIDTH + 2 * SSD_BC
DT_PAD = 128
IN_PROJ_PAD = S5_WIDTH + FNET_WIDTH + SSD_WIDTH + SSD_CONV_DIM + DT_PAD
D_FF = 2816
EPS = 1e-6
LOG2_E = math.log2(math.e)

LANES = 128
SUBLANES = 8
VMEM_LIMIT = 56 * 1024 * 1024

TOKEN_TILE = 512
FFN_TILE = 512
S5_CHUNK = 256
S5_SEQS = 2
SSD_CHUNK = 128
SSD_SUB = 4
FNET_L2 = 128
FNET_STAGE1_ROWS = 512


def _rmsnorm(x, g):
    return x * lax.rsqrt(jnp.mean(x * x, axis=-1, keepdims=True) + EPS) * g


def _split3(v):
    hi = v.astype(BF16)
    r = v - hi.astype(F32)
    mid = r.astype(BF16)
    lo = (r - mid.astype(F32)).astype(BF16)
    return hi, mid, lo


def _dot(a, b):
    return jnp.dot(a, b, preferred_element_type=F32)


def _dot_exact_lhs(m, v):
    hi, mid, lo = _split3(v)
    return _dot(m, hi) + _dot(m, mid) + _dot(m, lo)


def _dot_exact_rhs(v, m):
    hi, mid, lo = _split3(v)
    return _dot(hi, m) + _dot(mid, m) + _dot(lo, m)


def _dot_hilo(m_hi, m_lo, v):
    v_hi = v.astype(BF16)
    v_lo = (v - v_hi.astype(F32)).astype(BF16)
    return _dot(m_hi, v_hi) + _dot(m_hi, v_lo) + _dot(m_lo, v_hi)


def _const_spec(shape):
    nd = len(shape)
    return pl.BlockSpec(shape, lambda *_: (0,) * nd, pipeline_mode=pl.Buffered(1))


def _params(semantics):
    return pltpu.CompilerParams(dimension_semantics=semantics, vmem_limit_bytes=VMEM_LIMIT)


_INPROJ_SPLITS = (0, S5_WIDTH, S5_WIDTH + FNET_WIDTH, S5_WIDTH + FNET_WIDTH + SSD_WIDTH,
                  S5_WIDTH + FNET_WIDTH + SSD_WIDTH + SSD_CONV_DIM, IN_PROJ_PAD)


def _norm_inproj_kernel(prev_ref, x_ref, next_ref, g_ref, w_ref, cw_ref, cb_ref,
                        u_ref, v_ref, z_ref, xbc_ref, dt_ref, ext, *, tiles_per_seq):
    i = pl.program_id(0)
    t_len = x_ref.shape[0]
    pad = SSD_CONV // 2
    g = g_ref[...]
    h = _rmsnorm(x_ref[...], g).astype(BF16)
    lo_xbc, hi_xbc = _INPROJ_SPLITS[3], _INPROJ_SPLITS[4]
    w_xbc = w_ref[:, lo_xbc:hi_xbc]
    first = i % tiles_per_seq == 0
    last = i % tiles_per_seq == tiles_per_seq - 1
    h_prev = _rmsnorm(prev_ref[...], g).astype(BF16)
    h_next = _rmsnorm(next_ref[...], g).astype(BF16)
    ext[0:SUBLANES] = jnp.where(first, 0.0, _dot(h_prev, w_xbc))
    ext[SUBLANES:SUBLANES + t_len] = _dot(h, w_xbc)
    ext[SUBLANES + t_len:2 * SUBLANES + t_len] = jnp.where(last, 0.0, _dot(h_next, w_xbc))
    for o_ref, lo, hi in zip((u_ref, None, z_ref, None, dt_ref), _INPROJ_SPLITS[:-1], _INPROJ_SPLITS[1:]):
        if o_ref is not None:
            o_ref[...] = _dot(h, w_ref[:, lo:hi])
    v = _dot(h, w_ref[:, _INPROJ_SPLITS[1]:_INPROJ_SPLITS[2]])
    for hf in range(FNET_WIDTH // LANES):
        v_ref[hf] = v[:, hf * LANES:(hf + 1) * LANES]
    rows = t_len + 2 * SUBLANES
    xe = ext[...]
    acc = jnp.broadcast_to(cb_ref[...], (t_len, SSD_CONV_DIM))
    for k in range(SSD_CONV):
        tap = xe if k == pad else pltpu.roll(xe, (pad - k) % rows, axis=0)
        acc = acc + cw_ref[k:k + 1, :] * tap[SUBLANES:SUBLANES + t_len]
    xbc_ref[...] = acc * jax.nn.sigmoid(acc)


def _norm_inproj(x, g, w, conv_w, conv_b, seq_len):
    n = x.shape[0]
    widths = [hi - lo for lo, hi in zip(_INPROJ_SPLITS[:-1], _INPROJ_SPLITS[1:])]
    per = TOKEN_TILE // SUBLANES
    last = n // SUBLANES - 1
    halves = FNET_WIDTH // LANES
    out_shape = [jax.ShapeDtypeStruct((n, wd), F32) for wd in widths]
    out_specs = [pl.BlockSpec((TOKEN_TILE, wd), lambda i: (i, 0)) for wd in widths]
    out_shape[1] = jax.ShapeDtypeStruct((halves, n, LANES), F32)
    out_specs[1] = pl.BlockSpec((halves, TOKEN_TILE, LANES), lambda i: (0, i, 0))
    return pl.pallas_call(
        functools.partial(_norm_inproj_kernel, tiles_per_seq=seq_len // TOKEN_TILE),
        out_shape=out_shape,
        grid=(n // TOKEN_TILE,),
        in_specs=[pl.BlockSpec((SUBLANES, D_MODEL), lambda i: (jnp.maximum(i * per - 1, 0), 0)),
                  pl.BlockSpec((TOKEN_TILE, D_MODEL), lambda i: (i, 0)),
                  pl.BlockSpec((SUBLANES, D_MODEL), lambda i: (jnp.minimum((i + 1) * per, last), 0)),
                  _const_spec((1, D_MODEL)), _const_spec((D_MODEL, IN_PROJ_PAD)),
                  _const_spec(conv_w.shape), _const_spec(conv_b.shape)],
        out_specs=out_specs,
        scratch_shapes=[pltpu.VMEM((TOKEN_TILE + 2 * SUBLANES, SSD_CONV_DIM), F32)],
        compiler_params=_params(("parallel",)),
        name="norm_inproj",
    )(x, x, x, g, w, conv_w, conv_b)


def _s5_scan_kernel(uf_ref, ub_ref, bmat_ref, cmat_ref, lam_ref, yf_ref, yb_ref, bu0, bu1, h0, h1, carry):
    s = pl.program_id(1)
    nseq, t_len = uf_ref.shape[0], uf_ref.shape[1] // 2
    chains = [(i, d) for i in range(nseq) for d in range(2)]
    lam = [lam_ref[k] for k in range(4)]

    @pl.when(s == 0)
    def _():
        carry[...] = jnp.zeros_like(carry)
        bu1[...] = jnp.zeros_like(bu1)
        h0[...] = jnp.zeros_like(h0)

    def rows_of(half, d):
        lo = half * t_len if d == 0 else (1 - half) * t_len
        return pl.ds(lo, t_len)

    def project(half, bu, i, d):
        u16 = (uf_ref if d == 0 else ub_ref)[i, rows_of(half, d), :].astype(BF16)
        for part in range(2):
            col = (2 * d + part) * S5_NSTATE
            prod = _dot(u16, bmat_ref[:, col:col + S5_NSTATE])
            k = (i * 2 + d) * 2 + part
            for j in range(SUBLANES):
                bu[k, pl.ds(j, t_len, stride=SUBLANES), :] = prod[:, j * LANES:(j + 1) * LANES]

    def scan(bu, hb, hs, t_lo, t_hi):
        for t in range(t_lo, t_hi):
            for n, (i, d) in enumerate(chains):
                kre = (i * 2 + d) * 2
                row = pl.ds((t if d == 0 else t_len - 1 - t) * SUBLANES, SUBLANES)
                a_re, a_im = lam[2 * d], lam[2 * d + 1]
                h_re, h_im = hs[2 * n], hs[2 * n + 1]
                hs[2 * n] = a_re * h_re - a_im * h_im + bu[kre, row, :]
                hs[2 * n + 1] = a_re * h_im + a_im * h_re + bu[kre + 1, row, :]
                hb[kre, row, :] = hs[2 * n]
                hb[kre + 1, row, :] = hs[2 * n + 1]

    def read_out(half, hb, i, d):
        kre = (i * 2 + d) * 2
        h = jnp.concatenate(
            [hb[kre + part, pl.ds(j, t_len, stride=SUBLANES), :] for part in range(2) for j in range(SUBLANES)],
            axis=1).astype(BF16)
        (yf_ref if d == 0 else yb_ref)[i, rows_of(half, d), :] = _dot(h, cmat_ref[d])

    hs = [carry[k] for k in range(4 * nseq)]
    per = t_len // len(chains)
    for half, (bu_w, bu_r, h_w, h_r) in enumerate(((bu0, bu1, h1, h0), (bu1, bu0, h0, h1))):
        for n, (i, d) in enumerate(chains):
            project(half, bu_w, i, d)
            scan(bu_r, h_w, hs, n * per, (n + 1) * per)
            read_out(half, h_r, i, d)
    for k in range(4 * nseq):
        carry[k] = hs[k]


def _s5_scan(u, bmat, cmat, lam):
    b, l, _ = u.shape
    nb = l // (2 * S5_CHUNK)
    blk = (S5_SEQS, 2 * S5_CHUNK, S5_WIDTH)
    in_f = pl.BlockSpec(blk, lambda i, s: (i, jnp.minimum(s, nb - 1), 0))
    in_b = pl.BlockSpec(blk, lambda i, s: (i, nb - 1 - jnp.minimum(s, nb - 1), 0))
    out_f = pl.BlockSpec(blk, lambda i, s: (i, jnp.maximum(s - 1, 0), 0))
    out_b = pl.BlockSpec(blk, lambda i, s: (i, nb - 1 - jnp.maximum(s - 1, 0), 0))
    buf = pltpu.VMEM((S5_SEQS * 4, S5_CHUNK * SUBLANES, LANES), F32)
    return pl.pallas_call(
        _s5_scan_kernel,
        out_shape=[jax.ShapeDtypeStruct(u.shape, F32)] * 2,
        grid=(b // S5_SEQS, nb + 1),
        in_specs=[in_f, in_b, _const_spec(bmat.shape), _const_spec(cmat.shape), _const_spec(lam.shape)],
        out_specs=[out_f, out_b],
        scratch_shapes=[buf, buf, buf, buf, pltpu.VMEM((S5_SEQS * 4, SUBLANES, LANES), F32)],
        compiler_params=_params(("parallel", "arbitrary")),
        name="s5_scan",
    )(u, u, bmat, cmat, lam)


def _s5_prepare(b_re, b_im, dirs):
    g, p, c = S5_GROUPS, S5_STATE, S5_GROUP
    eye = jnp.eye(g, dtype=F32)
    bcols, cmats, lams = [], [], []
    for lam_re, lam_im, log_step, c_re, c_im in dirs:
        step = jnp.exp(log_step)[:, None]
        mag = jnp.exp(lam_re * step)
        lb_re, lb_im = mag * jnp.cos(lam_im * step), mag * jnp.sin(lam_im * step)
        den = lam_re * lam_re + lam_im * lam_im
        q_re = ((lb_re - 1.0) * lam_re + lb_im * lam_im) / den
        q_im = (lb_im * lam_re - (lb_re - 1.0) * lam_im) / den
        bb_re = q_re[..., None] * b_re - q_im[..., None] * b_im
        bb_im = q_re[..., None] * b_im + q_im[..., None] * b_re
        for bb in (bb_re, bb_im):
            bcols.append(jnp.einsum('gpc,gh->gchp', bb, eye).reshape(g * c, g * p))
        cm = [jnp.einsum('gcp,gh->gphc', cc, eye).reshape(g * p, g * c) for cc in (c_re, -c_im)]
        cmats.append(jnp.concatenate(cm, axis=0))
        lams += [lb_re.reshape(SUBLANES, LANES), lb_im.reshape(SUBLANES, LANES)]
    return (jnp.concatenate(bcols, axis=1).astype(BF16), jnp.stack(cmats).astype(BF16), jnp.stack(lams))


def _dft_parts(n, scale):
    k = np.arange(n)
    ang = 2.0 * np.pi * ((k[:, None] * k[None, :]) % n) / n
    return np.cos(ang) * scale, np.sin(ang) * scale


def _hilo(m):
    m = np.asarray(m, np.float32)
    hi = jnp.asarray(m, F32).astype(BF16)
    lo = (jnp.asarray(m, F32) - hi.astype(F32)).astype(BF16)
    return hi, lo


def _fnet_fold_kernel(ch_ref, cl_ref, sh_ref, sl_ref, w_ref, a_ref, b_ref):
    w = w_ref[...]
    a_ref[...] = _dot_hilo(ch_ref[...], cl_ref[...], w).astype(BF16)
    b_ref[...] = _dot_hilo(sh_ref[...], sl_ref[...], w).astype(BF16)


def _fnet_fold(w):
    c64, s64 = _dft_parts(FNET_GROUP, 1.0 / math.sqrt(FNET_GROUP))
    eye = np.eye(FNET_GROUPS)
    cbd, sbd = np.kron(eye, c64), np.kron(eye, s64)
    wbd = jnp.einsum('gcd,gh->gchd', w, jnp.eye(FNET_GROUPS, dtype=F32)).reshape(FNET_WIDTH, FNET_WIDTH)
    shape = (FNET_WIDTH, FNET_WIDTH)
    return pl.pallas_call(
        _fnet_fold_kernel,
        out_shape=[jax.ShapeDtypeStruct(shape, BF16)] * 2,
        name="fnet_fold",
    )(*_hilo(cbd), *_hilo(sbd), wbd)


def _fnet_stage1_kernel(x_ref, fh_ref, fl_ref, gre_ref, gim_ref):
    halves, l1, nb = x_ref.shape[0], x_ref.shape[1], x_ref.shape[2]
    order = [(hf, s) for hf in range(halves) for s in range(nb)]
    g = _dot_hilo(fh_ref[...], fl_ref[...], jnp.concatenate([x_ref[hf, :, s, :] for hf, s in order], axis=1))
    for idx, (hf, s) in enumerate(order):
        gre_ref[hf, :, s, :] = g[:l1, idx * LANES:(idx + 1) * LANES]
        gim_ref[hf, :, s, :] = g[l1:, idx * LANES:(idx + 1) * LANES]


def _fnet_stage2_kernel(gre_ref, gim_ref, twr_ref, twi_ref, fh_ref, fl_ref, a_ref, b_ref, bias_ref, o_ref):
    halves, kb, l2 = gre_ref.shape[0], gre_ref.shape[1], gre_ref.shape[2]
    cols = []
    for i in range(kb):
        tr, ti = twr_ref[i], twi_ref[i]
        for hf in range(halves):
            gr, gi = gre_ref[hf, i], gim_ref[hf, i]
            cols.append(jnp.concatenate([gr * tr - gi * ti, gr * ti + gi * tr], axis=0))
    y = _dot_hilo(fh_ref[...], fl_ref[...], jnp.concatenate(cols, axis=1))
    y_re = jnp.concatenate([y[:l2, i * FNET_WIDTH:(i + 1) * FNET_WIDTH] for i in range(kb)], axis=0)
    y_im = jnp.concatenate([y[l2:, i * FNET_WIDTH:(i + 1) * FNET_WIDTH] for i in range(kb)], axis=0)
    out = _dot(y_re.astype(BF16), a_ref[...]) + _dot(y_im.astype(BF16), b_ref[...]) + bias_ref[...]
    for i in range(kb):
        for hf in range(halves):
            o_ref[hf, :, i, :] = out[i * l2:(i + 1) * l2, hf * LANES:(hf + 1) * LANES]


def _fnet(v, b, a, bm, bias):
    halves, n, _ = v.shape
    l = n // b
    l2 = FNET_L2
    l1 = l // l2
    c1, s1 = _dft_parts(l1, 1.0 / math.sqrt(l1))
    f1h, f1l = _hilo(np.concatenate([c1, -s1], axis=0))
    c2, s2 = _dft_parts(l2, 1.0 / math.sqrt(l2))
    f2h, f2l = _hilo(np.block([[c2, s2], [-s2, c2]]))
    m = (np.arange(l1)[:, None] * np.arange(l2)[None, :]) % l
    ang = 2.0 * np.pi * m / l
    twr = jnp.broadcast_to(jnp.asarray(np.cos(ang), F32)[:, :, None], (l1, l2, LANES))
    twi = jnp.broadcast_to(jnp.asarray(-np.sin(ang), F32)[:, :, None], (l1, l2, LANES))

    nb = FNET_STAGE1_ROWS // l1
    xs = pl.BlockSpec((halves, None, l1, nb, LANES), lambda i, j: (0, i, 0, j, 0))
    g_shape = jax.ShapeDtypeStruct((halves, b, l1, l2, LANES), F32)
    gre, gim = pl.pallas_call(
        _fnet_stage1_kernel,
        out_shape=[g_shape, g_shape],
        grid=(b, l2 // nb),
        in_specs=[xs, _const_spec(f1h.shape), _const_spec(f1l.shape)],
        out_specs=[xs, xs],
        compiler_params=_params(("parallel", "parallel")),
        name="fnet_stage1",
    )(v.reshape(halves, b, l1, l2, LANES), f1h, f1l)

    kb = SUBLANES
    gs = pl.BlockSpec((halves, None, kb, l2, LANES), lambda i, j: (0, i, j, 0, 0))
    ts = pl.BlockSpec((kb, l2, LANES), lambda i, j: (j, 0, 0))
    y = pl.pallas_call(
        _fnet_stage2_kernel,
        out_shape=jax.ShapeDtypeStruct((halves, b, l2, l1, LANES), F32),
        grid=(b, l1 // kb),
        in_specs=[gs, gs, ts, ts, _const_spec(f2h.shape), _const_spec(f2l.shape),
                  _const_spec(a.shape), _const_spec(bm.shape), _const_spec(bias.shape)],
        out_specs=pl.BlockSpec((halves, None, l2, kb, LANES), lambda i, j: (0, i, 0, j, 0)),
        compiler_params=_params(("parallel", "parallel")),
        name="fnet_stage2",
    )(gre, gim, twr, twi, f2h, f2l, a, bm, bias)
    return y.reshape(halves, n, LANES)


def _ones_where(mask):
    return jnp.where(mask, 1.0, 0.0).astype(BF16)


def _softplus(x):
    return jnp.maximum(x, 0.0) + jnp.log1p(jnp.exp(-jnp.abs(x)))


def _ssd_dt_selector():
    sel = np.zeros((2, LANES, SSD_WIDTH), np.float32)
    for d in range(2):
        for lane in range(SSD_WIDTH):
            sel[d, 2 * d * SSD_HEADS + lane // SSD_HEAD_DIM, lane] = 1.0
    return jnp.asarray(sel, BF16)


def _ssd_scan_kernel(xf_ref, xb_ref, dtf_ref, dtb_ref, part_ref, sel_ref, yf_ref, yb_ref, state):
    c = pl.program_id(1)
    t_len = SSD_CHUNK
    n_sub = xf_ref.shape[1] // t_len
    pair_w = 2 * SSD_HEAD_DIM
    heads_per_pair = pair_w // SSD_HEAD_DIM
    n_pairs = SSD_WIDTH // pair_w
    pairs_per_group = n_pairs // SSD_GROUPS

    @pl.when(c == 0)
    def _():
        state[...] = jnp.zeros_like(state)

    row = lax.broadcasted_iota(jnp.int32, (t_len, t_len), 0)
    col = lax.broadcasted_iota(jnp.int32, (t_len, t_len), 1)
    lower, upper = _ones_where(row >= col), _ones_where(row <= col)
    first_head = lax.broadcasted_iota(jnp.int32, (1, pair_w), 1) < SSD_HEAD_DIM
    states = [[state[d, q] for q in range(n_pairs)] for d in range(2)]

    for k in range(n_sub):
        at = [k * t_len, (n_sub - 1 - k) * t_len]
        rows = []
        for d, dt_ref in enumerate((dtf_ref, dtb_ref)):
            dt_row = _softplus(dt_ref[0, at[d]:at[d] + t_len, :].T[0:SSD_HEADS] + part_ref[:, d:d + 1])
            rows += [dt_row, dt_row * (-LOG2_E * jnp.exp(part_ref[:, 2 + d:3 + d]))]
        n_rows = len(rows) * SSD_HEADS
        row_form = jnp.concatenate(rows, axis=0)
        col_form = jnp.concatenate([row_form, jnp.zeros((LANES - n_rows, t_len), F32)], axis=0).T
        cum_col = _dot_exact_lhs(lower, col_form)
        cum_row = _dot_exact_rhs(row_form, upper)
        cf_hi = col_form.astype(BF16)
        cf_lo = (col_form - cf_hi.astype(F32)).astype(BF16)

        for d, (x_ref, y_ref) in enumerate(((xf_ref, yf_ref), (xb_ref, yb_ref))):
            da_at = (2 * d + 1) * SSD_HEADS
            if d == 0:
                keep = row >= col
                p_col = cum_col
                p_row = cum_row[da_at:da_at + SSD_HEADS]
            else:
                keep = col >= row
                p_col = col_form - cum_col
                p_row = row_form[da_at:da_at + SSD_HEADS] - cum_row[da_at:da_at + SSD_HEADS]

            xbc = x_ref[0, at[d]:at[d] + t_len, :]
            dt_wide = _dot(cf_hi, sel_ref[d]) + _dot(cf_lo, sel_ref[d])
            for g in range(SSD_GROUPS):
                b_g = xbc[:, SSD_WIDTH + g * SSD_STATE:SSD_WIDTH + (g + 1) * SSD_STATE]
                c_lo = SSD_WIDTH + SSD_BC + g * SSD_STATE
                c_g = xbc[:, c_lo:c_lo + SSD_STATE].astype(BF16)
                bt_g = b_g.T.astype(BF16)
                cb = _dot(c_g, bt_g)
                for q in range(g * pairs_per_group, (g + 1) * pairs_per_group):
                    h0 = q * heads_per_pair

                    def expand(m):
                        return jnp.where(first_head, m[:, da_at + h0:da_at + h0 + 1],
                                         m[:, da_at + h0 + 1:da_at + h0 + 2])

                    p_e = expand(p_col)
                    xdt = xbc[:, q * pair_w:(q + 1) * pair_w] * dt_wide[:, q * pair_w:(q + 1) * pair_w]
                    if d == 0:
                        total = p_e[t_len - 1:t_len]
                        off_scale = jnp.exp2(p_e)
                        w_state = jnp.exp2(total - p_e)
                    else:
                        total = expand(col_form[t_len - 1:t_len]) - p_e[t_len - 1:t_len]
                        off_scale = jnp.exp2(total + p_e)
                        w_state = jnp.exp2(-p_e)
                    st = states[d][q]
                    y = _dot(c_g, st.astype(BF16)) * off_scale
                    xdt16 = xdt.astype(BF16)
                    intra = []
                    for h in range(h0, h0 + heads_per_pair):
                        diff = p_col[:, da_at + h:da_at + h + 1] - p_row[h:h + 1, :]
                        m = cb * jnp.exp2(jnp.where(keep, diff, -jnp.inf))
                        intra.append(_dot(m.astype(BF16), xdt16))
                    y_ref[0, at[d]:at[d] + t_len, q * pair_w:(q + 1) * pair_w] = (
                        y + jnp.where(first_head, intra[0], intra[1]))
                    states[d][q] = st * jnp.exp2(total) + _dot(bt_g, (xdt * w_state).astype(BF16))

    for d in range(2):
        for q in range(n_pairs):
            state[d, q] = states[d][q]


def _ssd_scan(xbc, dt, part):
    b, l, ch = xbc.shape
    t_blk = SSD_CHUNK * SSD_SUB
    nb = l // t_blk
    pairs = SSD_WIDTH // (2 * SSD_HEAD_DIM)
    sel = _ssd_dt_selector()

    def fwd(width):
        return pl.BlockSpec((1, t_blk, width), lambda i, c: (i, c, 0))

    def bwd(width):
        return pl.BlockSpec((1, t_blk, width), lambda i, c: (i, nb - 1 - c, 0))

    return pl.pallas_call(
        _ssd_scan_kernel,
        out_shape=[jax.ShapeDtypeStruct((b, l, SSD_WIDTH), F32)] * 2,
        grid=(b, nb),
        in_specs=[fwd(ch), bwd(ch), fwd(DT_PAD), bwd(DT_PAD), _const_spec(part.shape), _const_spec(sel.shape)],
        out_specs=[fwd(SSD_WIDTH), bwd(SSD_WIDTH)],
        scratch_shapes=[pltpu.VMEM((2, pairs, SSD_STATE, 2 * SSD_HEAD_DIM), F32)],
        compiler_params=_params(("parallel", "arbitrary")),
        name="ssd_scan",
    )(xbc, xbc, dt, dt, part, sel)


def _mix_ffn_kernel(x_ref, s5f_ref, s5b_ref, u_ref, fn_ref, sdf_ref, sdb_ref, xc_ref, z_ref,
                    s5d_ref, wglu_ref, bglu_ref, sdd_ref, sdg_ref, wo_ref,
                    g_ref, wg_ref, wu_ref, wd_ref, gf_ref, o_ref, *, final_norm):
    ya = s5f_ref[...] + s5b_ref[...] + u_ref[...] * s5d_ref[...]
    ga = jax.nn.gelu(ya)
    ya = ga * jax.nn.sigmoid(_dot(ga.astype(BF16), wglu_ref[...]) + bglu_ref[...])
    z = z_ref[...]
    yc = (sdf_ref[...] + sdb_ref[...] + xc_ref[...] * sdd_ref[...]) * (z * jax.nn.sigmoid(z))
    yc = _rmsnorm(yc, sdg_ref[...])
    lo, mid = S5_WIDTH, S5_WIDTH + FNET_WIDTH
    x = (x_ref[...] + _dot(ya.astype(BF16), wo_ref[0:lo, :])
         + _dot(jnp.concatenate([fn_ref[hf] for hf in range(fn_ref.shape[0])], axis=1).astype(BF16), wo_ref[lo:mid, :])
         + _dot(yc.astype(BF16), wo_ref[mid:, :]))
    h = _rmsnorm(x, g_ref[...]).astype(BF16)
    gate = _dot(h, wg_ref[...])
    act = (gate * jax.nn.sigmoid(gate) * _dot(h, wu_ref[...])).astype(BF16)
    y = x + _dot(act, wd_ref[...])
    o_ref[...] = _rmsnorm(y, gf_ref[...]) if final_norm else y


def _mix_ffn(x, s5f, s5b, u, fn, sdf, sdb, xbc_c, z, consts, final_norm):
    n = x.shape[0]

    def rows(width):
        return pl.BlockSpec((FFN_TILE, width), lambda i: (i, 0))

    return pl.pallas_call(
        functools.partial(_mix_ffn_kernel, final_norm=final_norm),
        out_shape=jax.ShapeDtypeStruct(x.shape, F32),
        grid=(n // FFN_TILE,),
        in_specs=[rows(D_MODEL), rows(S5_WIDTH), rows(S5_WIDTH), rows(S5_WIDTH),
                  pl.BlockSpec((fn.shape[0], FFN_TILE, LANES), lambda i: (0, i, 0)),
                  rows(SSD_WIDTH), rows(SSD_WIDTH), rows(SSD_WIDTH), rows(SSD_WIDTH)]
                 + [_const_spec(a.shape) for a in consts],
        out_specs=rows(D_MODEL),
        compiler_params=_params(("parallel",)),
        name="mix_ffn",
    )(x, s5f, s5b, u, fn, sdf, sdb, xbc_c, z, *consts)


def _row(v):
    return v.reshape(1, -1).astype(F32)


def _prepare_layer(i, p):
    lay = {}
    lay['norm_mix_g'] = _row(p['norm_mix_g'][i])
    lay['w_in'] = jnp.pad(p['w_in'][i], ((0, 0), (0, IN_PROJ_PAD - p['w_in'].shape[-1]))).astype(BF16)
    lay['s5'] = _s5_prepare(
        p['s5_b_re'][i], p['s5_b_im'][i],
        [(p['s5_lam_re_f'][i], p['s5_lam_im_f'][i], p['s5_log_step_f'][i], p['s5_c_re_f'][i], p['s5_c_im_f'][i]),
         (p['s5_lam_re_b'][i], p['s5_lam_im_b'][i], p['s5_log_step_b'][i], p['s5_c_re_b'][i], p['s5_c_im_b'][i])])
    lay['s5_d'] = _row(p['s5_d'][i])
    lay['s5_w_glu'] = p['s5_w_glu'][i].astype(BF16)
    lay['s5_b_glu'] = _row(p['s5_b_glu'][i])
    lay['fnet_ab'] = _fnet_fold(p['fnet_w'][i])
    lay['fnet_b'] = _row(p['fnet_b'][i])
    lay['conv_w'] = p['ssd_conv_w'][i].astype(F32)
    lay['conv_b'] = _row(p['ssd_conv_b'][i])
    par = jnp.stack([p['ssd_dt_bias_f'][i], p['ssd_dt_bias_b'][i], p['ssd_a_log_f'][i], p['ssd_a_log_b'][i]]).astype(F32)
    lay['ssd_part'] = par.T
    lay['ssd_d'] = _row(jnp.repeat(p['ssd_d'][i], SSD_HEAD_DIM))
    lay['ssd_norm_g'] = _row(p['ssd_norm_g'][i])
    lay['w_out'] = p['w_out'][i].astype(BF16)
    lay['norm_ffn_g'] = _row(p['norm_ffn_g'][i])
    lay['w_gate'] = p['w_gate'][i].astype(BF16)
    lay['w_up'] = p['w_up'][i].astype(BF16)
    lay['w_down'] = p['w_down'][i].astype(BF16)
    return lay


def _trunk(x3, layers, final_g):
    b, l, dm = x3.shape
    x = x3.reshape(b * l, dm)
    for i, lay in enumerate(layers):
        u, v, z, xbc_c, dt = _norm_inproj(x, lay['norm_mix_g'], lay['w_in'], lay['conv_w'], lay['conv_b'], l)
        s5f, s5b = _s5_scan(u.reshape(b, l, -1), *lay['s5'])
        fn = _fnet(v, b, *lay['fnet_ab'], lay['fnet_b'])
        sdf, sdb = _ssd_scan(xbc_c.reshape(b, l, -1), dt.reshape(b, l, -1), lay['ssd_part'])
        n = b * l
        consts = (lay['s5_d'], lay['s5_w_glu'], lay['s5_b_glu'], lay['ssd_d'], lay['ssd_norm_g'], lay['w_out'],
                  lay['norm_ffn_g'], lay['w_gate'], lay['w_up'], lay['w_down'], final_g)
        x = _mix_ffn(x, s5f.reshape(n, -1), s5b.reshape(n, -1), u, fn,
                     sdf.reshape(n, -1), sdb.reshape(n, -1), xbc_c, z, consts,
                     final_norm=(i == len(layers) - 1))
    return x.reshape(b, l, dm)


def kernel(x_prompt, x_sample, norm_mix_g, w_in, s5_b_re, s5_b_im, s5_lam_re_f, s5_lam_im_f, s5_log_step_f, s5_c_re_f, s5_c_im_f, s5_lam_re_b, s5_lam_im_b, s5_log_step_b, s5_c_re_b, s5_c_im_b, s5_d, s5_w_glu, s5_b_glu, fnet_w, fnet_b, ssd_conv_w, ssd_conv_b, ssd_a_log_f, ssd_dt_bias_f, ssd_a_log_b, ssd_dt_bias_b, ssd_d, ssd_norm_g, w_out, norm_ffn_g, w_gate, w_up, w_down, final_norm_g):
    p = dict(norm_mix_g=norm_mix_g, w_in=w_in, s5_b_re=s5_b_re, s5_b_im=s5_b_im,
             s5_lam_re_f=s5_lam_re_f, s5_lam_im_f=s5_lam_im_f, s5_log_step_f=s5_log_step_f,
             s5_c_re_f=s5_c_re_f, s5_c_im_f=s5_c_im_f,
             s5_lam_re_b=s5_lam_re_b, s5_lam_im_b=s5_lam_im_b, s5_log_step_b=s5_log_step_b,
             s5_c_re_b=s5_c_re_b, s5_c_im_b=s5_c_im_b,
             s5_d=s5_d, s5_w_glu=s5_w_glu, s5_b_glu=s5_b_glu, fnet_w=fnet_w, fnet_b=fnet_b,
             ssd_conv_w=ssd_conv_w, ssd_conv_b=ssd_conv_b, ssd_a_log_f=ssd_a_log_f,
             ssd_dt_bias_f=ssd_dt_bias_f, ssd_a_log_b=ssd_a_log_b, ssd_dt_bias_b=ssd_dt_bias_b,
             ssd_d=ssd_d, ssd_norm_g=ssd_norm_g, w_out=w_out, norm_ffn_g=norm_ffn_g,
             w_gate=w_gate, w_up=w_up, w_down=w_down)
    layers = [_prepare_layer(i, p) for i in range(norm_mix_g.shape[0])]
    final_g = _row(final_norm_g)
    return (_trunk(x_prompt, layers, final_g), _trunk(x_sample, layers, final_g))
```

```python
import functools
import math

import numpy as np
import jax
import jax.numpy as jnp
from jax import lax
from jax.experimental import pallas as pl
from jax.experimental.pallas import tpu as pltpu

F32 = jnp.float32
BF16 = jnp.bfloat16

D_MODEL = 1024
S5_WIDTH = 256
S5_GROUP = 16
S5_GROUPS = 16
S5_STATE = 64
S5_NSTATE = S5_GROUPS * S5_STATE
FNET_WIDTH = 256
FNET_GROUPS = 4
FNET_GROUP = 64
SSD_WIDTH = 512
SSD_HEAD_DIM = 64
SSD_HEADS = 8
SSD_GROUPS = 2
SSD_STATE = 128
SSD_CONV = 5
SSD_BC = SSD_GROUPS * SSD_STATE
SSD_CONV_DIM = SSD_WIDTH + 2 * SSD_BC
DT_PAD = 128
IN_PROJ_PAD = S5_WIDTH + FNET_WIDTH + SSD_WIDTH + SSD_CONV_DIM + DT_PAD
D_FF = 2816
EPS = 1e-6
LOG2_E = math.log2(math.e)

LANES = 128
SUBLANES = 8
VMEM_LIMIT = 56 * 1024 * 1024

TOKEN_TILE = 512
FFN_TILE = 512
S5_CHUNK = 256
S5_SEQS = 2
SSD_CHUNK = 128
SSD_SUB = 4
FNET_L2 = 128
FNET_STAGE1_ROWS = 512


def _rmsnorm(x, g):
    return x * lax.rsqrt(jnp.mean(x * x, axis=-1, keepdims=True) + EPS) * g


def _split3(v):
    hi = v.astype(BF16)
    r = v - hi.astype(F32)
    mid = r.astype(BF16)
    lo = (r - mid.astype(F32)).astype(BF16)
    return hi, mid, lo


def _dot(a, b):
    return jnp.dot(a, b, preferred_element_type=F32)


def _dot_exact_lhs(m, v):
    hi, mid, lo = _split3(v)
    return _dot(m, hi) + _dot(m, mid) + _dot(m, lo)


def _dot_exact_rhs(v, m):
    hi, mid, lo = _split3(v)
    return _dot(hi, m) + _dot(mid, m) + _dot(lo, m)


def _dot_hilo(m_hi, m_lo, v):
    v_hi = v.astype(BF16)
    v_lo = (v - v_hi.astype(F32)).astype(BF16)
    return _dot(m_hi, v_hi) + _dot(m_hi, v_lo) + _dot(m_lo, v_hi)


def _const_spec(shape):
    nd = len(shape)
    return pl.BlockSpec(shape, lambda *_: (0,) * nd, pipeline_mode=pl.Buffered(1))


def _params(semantics):
    return pltpu.CompilerParams(dimension_semantics=semantics, vmem_limit_bytes=VMEM_LIMIT)


_INPROJ_SPLITS = (0, S5_WIDTH, S5_WIDTH + FNET_WIDTH, S5_WIDTH + FNET_WIDTH + SSD_WIDTH,
                  S5_WIDTH + FNET_WIDTH + SSD_WIDTH + SSD_CONV_DIM, IN_PROJ_PAD)


def _norm_inproj_kernel(prev_ref, x_ref, next_ref, g_ref, w_ref, cw_ref, cb_ref,
                        u_ref, v_ref, z_ref, xbc_ref, dt_ref, ext, *, tiles_per_seq):
    i = pl.program_id(0)
    t_len = x_ref.shape[0]
    pad = SSD_CONV // 2
    g = g_ref[...]
    h = _rmsnorm(x_ref[...], g).astype(BF16)
    lo_xbc, hi_xbc = _INPROJ_SPLITS[3], _INPROJ_SPLITS[4]
    w_xbc = w_ref[:, lo_xbc:hi_xbc]
    first = i % tiles_per_seq == 0
    last = i % tiles_per_seq == tiles_per_seq - 1
    h_prev = _rmsnorm(prev_ref[...], g).astype(BF16)
    h_next = _rmsnorm(next_ref[...], g).astype(BF16)
    ext[0:SUBLANES] = jnp.where(first, 0.0, _dot(h_prev, w_xbc))
    ext[SUBLANES:SUBLANES + t_len] = _dot(h, w_xbc)
    ext[SUBLANES + t_len:2 * SUBLANES + t_len] = jnp.where(last, 0.0, _dot(h_next, w_xbc))
    for o_ref, lo, hi in zip((u_ref, None, z_ref, None, dt_ref), _INPROJ_SPLITS[:-1], _INPROJ_SPLITS[1:]):
        if o_ref is not None:
            o_ref[...] = _dot(h, w_ref[:, lo:hi])
    v = _dot(h, w_ref[:, _INPROJ_SPLITS[1]:_INPROJ_SPLITS[2]])
    for hf in range(FNET_WIDTH // LANES):
        v_ref[hf] = v[:, hf * LANES:(hf + 1) * LANES]
    rows = t_len + 2 * SUBLANES
    xe = ext[...]
    acc = jnp.broadcast_to(cb_ref[...], (t_len, SSD_CONV_DIM))
    for k in range(SSD_CONV):
        tap = xe if k == pad else pltpu.roll(xe, (pad - k) % rows, axis=0)
        acc = acc + cw_ref[k:k + 1, :] * tap[SUBLANES:SUBLANES + t_len]
    xbc_ref[...] = acc * jax.nn.sigmoid(acc)


def _norm_inproj(x, g, w, conv_w, conv_b, seq_len):
    n = x.shape[0]
    widths = [hi - lo for lo, hi in zip(_INPROJ_SPLITS[:-1], _INPROJ_SPLITS[1:])]
    per = TOKEN_TILE // SUBLANES
    last = n // SUBLANES - 1
    halves = FNET_WIDTH // LANES
    out_shape = [jax.ShapeDtypeStruct((n, wd), F32) for wd in widths]
    out_specs = [pl.BlockSpec((TOKEN_TILE, wd), lambda i: (i, 0)) for wd in widths]
    out_shape[1] = jax.ShapeDtypeStruct((halves, n, LANES), F32)
    out_specs[1] = pl.BlockSpec((halves, TOKEN_TILE, LANES), lambda i: (0, i, 0))
    return pl.pallas_call(
        functools.partial(_norm_inproj_kernel, tiles_per_seq=seq_len // TOKEN_TILE),
        out_shape=out_shape,
        grid=(n // TOKEN_TILE,),
        in_specs=[pl.BlockSpec((SUBLANES, D_MODEL), lambda i: (jnp.maximum(i * per - 1, 0), 0)),
                  pl.BlockSpec((TOKEN_TILE, D_MODEL), lambda i: (i, 0)),
                  pl.BlockSpec((SUBLANES, D_MODEL), lambda i: (jnp.minimum((i + 1) * per, last), 0)),
                  _const_spec((1, D_MODEL)), _const_spec((D_MODEL, IN_PROJ_PAD)),
                  _const_spec(conv_w.shape), _const_spec(conv_b.shape)],
        out_specs=out_specs,
        scratch_shapes=[pltpu.VMEM((TOKEN_TILE + 2 * SUBLANES, SSD_CONV_DIM), F32)],
        compiler_params=_params(("parallel",)),
        name="norm_inproj",
    )(x, x, x, g, w, conv_w, conv_b)


def _s5_scan_kernel(uf_ref, ub_ref, bmat_ref, cmat_ref, lam_ref, yf_ref, yb_ref, bu0, bu1, h0, h1, carry):
    s = pl.program_id(1)
    nseq, t_len = uf_ref.shape[0], uf_ref.shape[1] // 2
    chains = [(i, d) for i in range(nseq) for d in range(2)]
    lam = [lam_ref[k] for k in range(4)]

    @pl.when(s == 0)
    def _():
        carry[...] = jnp.zeros_like(carry)
        bu1[...] = jnp.zeros_like(bu1)
        h0[...] = jnp.zeros_like(h0)

    def rows_of(half, d):
        lo = half * t_len if d == 0 else (1 - half) * t_len
        return pl.ds(lo, t_len)

    def project(half, bu, i, d):
        u16 = (uf_ref if d == 0 else ub_ref)[i, rows_of(half, d), :].astype(BF16)
        for part in range(2):
            col = (2 * d + part) * S5_NSTATE
            prod = _dot(u16, bmat_ref[:, col:col + S5_NSTATE])
            k = (i * 2 + d) * 2 + part
            for j in range(SUBLANES):
                bu[k, pl.ds(j, t_len, stride=SUBLANES), :] = prod[:, j * LANES:(j + 1) * LANES]

    def scan(bu, hb, hs, t_lo, t_hi):
        for t in range(t_lo, t_hi):
            for n, (i, d) in enumerate(chains):
                kre = (i * 2 + d) * 2
                row = pl.ds((t if d == 0 else t_len - 1 - t) * SUBLANES, SUBLANES)
                a_re, a_im = lam[2 * d], lam[2 * d + 1]
                h_re, h_im = hs[2 * n], hs[2 * n + 1]
                hs[2 * n] = a_re * h_re - a_im * h_im + bu[kre, row, :]
                hs[2 * n + 1] = a_re * h_im + a_im * h_re + bu[kre + 1, row, :]
                hb[kre, row, :] = hs[2 * n]
                hb[kre + 1, row, :] = hs[2 * n + 1]

    def read_out(half, hb, i, d):
        kre = (i * 2 + d) * 2
        h = jnp.concatenate(
            [hb[kre + part, pl.ds(j, t_len, stride=SUBLANES), :] for part in range(2) for j in range(SUBLANES)],
            axis=1).astype(BF16)
        (yf_ref if d == 0 else yb_ref)[i, rows_of(half, d), :] = _dot(h, cmat_ref[d])

    hs = [carry[k] for k in range(4 * nseq)]
    per = t_len // len(chains)
    for half, (bu_w, bu_r, h_w, h_r) in enumerate(((bu0, bu1, h1, h0), (bu1, bu0, h0, h1))):
        for n, (i, d) in enumerate(chains):
            project(half, bu_w, i, d)
            scan(bu_r, h_w, hs, n * per, (n + 1) * per)
            read_out(half, h_r, i, d)
    for k in range(4 * nseq):
        carry[k] = hs[k]


def _s5_scan(u, bmat, cmat, lam):
    b, l, _ = u.shape
    nb = l // (2 * S5_CHUNK)
    blk = (S5_SEQS, 2 * S5_CHUNK, S5_WIDTH)
    in_f = pl.BlockSpec(blk, lambda i, s: (i, jnp.minimum(s, nb - 1), 0))
    in_b = pl.BlockSpec(blk, lambda i, s: (i, nb - 1 - jnp.minimum(s, nb - 1), 0))
    out_f = pl.BlockSpec(blk, lambda i, s: (i, jnp.maximum(s - 1, 0), 0))
    out_b = pl.BlockSpec(blk, lambda i, s: (i, nb - 1 - jnp.maximum(s - 1, 0), 0))
    buf = pltpu.VMEM((S5_SEQS * 4, S5_CHUNK * SUBLANES, LANES), F32)
    return pl.pallas_call(
        _s5_scan_kernel,
        out_shape=[jax.ShapeDtypeStruct(u.shape, F32)] * 2,
        grid=(b // S5_SEQS, nb + 1),
        in_specs=[in_f, in_b, _const_spec(bmat.shape), _const_spec(cmat.shape), _const_spec(lam.shape)],
        out_specs=[out_f, out_b],
        scratch_shapes=[buf, buf, buf, buf, pltpu.VMEM((S5_SEQS * 4, SUBLANES, LANES), F32)],
        compiler_params=_params(("parallel", "arbitrary")),
        name="s5_scan",
    )(u, u, bmat, cmat, lam)


def _s5_prepare(b_re, b_im, dirs):
    g, p, c = S5_GROUPS, S5_STATE, S5_GROUP
    eye = jnp.eye(g, dtype=F32)
    bcols, cmats, lams = [], [], []
    for lam_re, lam_im, log_step, c_re, c_im in dirs:
        step = jnp.exp(log_step)[:, None]
        mag = jnp.exp(lam_re * step)
        lb_re, lb_im = mag * jnp.cos(lam_im * step), mag * jnp.sin(lam_im * step)
        den = lam_re * lam_re + lam_im * lam_im
        q_re = ((lb_re - 1.0) * lam_re + lb_im * lam_im) / den
        q_im = (lb_im * lam_re - (lb_re - 1.0) * lam_im) / den
        bb_re = q_re[..., None] * b_re - q_im[..., None] * b_im
        bb_im = q_re[..., None] * b_im + q_im[..., None] * b_re
        for bb in (bb_re, bb_im):
            bcols.append(jnp.einsum('gpc,gh->gchp', bb, eye).reshape(g * c, g * p))
        cm = [jnp.einsum('gcp,gh->gphc', cc, eye).reshape(g * p, g * c) for cc in (c_re, -c_im)]
        cmats.append(jnp.concatenate(cm, axis=0))
        lams += [lb_re.reshape(SUBLANES, LANES), lb_im.reshape(SUBLANES, LANES)]
    return (jnp.concatenate(bcols, axis=1).astype(BF16), jnp.stack(cmats).astype(BF16), jnp.stack(lams))


def _dft_parts(n, scale):
    k = np.arange(n)
    ang = 2.0 * np.pi * ((k[:, None] * k[None, :]) % n) / n
    return np.cos(ang) * scale, np.sin(ang) * scale


def _hilo(m):
    m = np.asarray(m, np.float32)
    hi = jnp.asarray(m, F32).astype(BF16)
    lo = (jnp.asarray(m, F32) - hi.astype(F32)).astype(BF16)
    return hi, lo


def _fnet_fold_kernel(ch_ref, cl_ref, sh_ref, sl_ref, w_ref, a_ref, b_ref):
    w = w_ref[...]
    a_ref[...] = _dot_hilo(ch_ref[...], cl_ref[...], w).astype(BF16)
    b_ref[...] = _dot_hilo(sh_ref[...], sl_ref[...], w).astype(BF16)


def _fnet_fold(w):
    c64, s64 = _dft_parts(FNET_GROUP, 1.0 / math.sqrt(FNET_GROUP))
    eye = np.eye(FNET_GROUPS)
    cbd, sbd = np.kron(eye, c64), np.kron(eye, s64)
    wbd = jnp.einsum('gcd,gh->gchd', w, jnp.eye(FNET_GROUPS, dtype=F32)).reshape(FNET_WIDTH, FNET_WIDTH)
    shape = (FNET_WIDTH, FNET_WIDTH)
    return pl.pallas_call(
        _fnet_fold_kernel,
        out_shape=[jax.ShapeDtypeStruct(shape, BF16)] * 2,
        name="fnet_fold",
    )(*_hilo(cbd), *_hilo(sbd), wbd)


def _fnet_stage1_kernel(x_ref, fh_ref, fl_ref, gre_ref, gim_ref, xs, gs):
    halves, l1, nb = x_ref.shape[0], x_ref.shape[1], x_ref.shape[2]
    for hf in range(halves):
        xs[hf] = x_ref[hf].reshape(l1 * nb, LANES)
    order = [(hf, s) for hf in range(halves) for s in range(nb)]
    g = _dot_hilo(fh_ref[...], fl_ref[...],
                  jnp.concatenate([xs[hf, pl.ds(s, l1, stride=nb), :] for hf, s in order], axis=1))
    for idx, (hf, s) in enumerate(order):
        gs[2 * hf, pl.ds(s, l1, stride=nb), :] = g[:l1, idx * LANES:(idx + 1) * LANES]
        gs[2 * hf + 1, pl.ds(s, l1, stride=nb), :] = g[l1:, idx * LANES:(idx + 1) * LANES]
    for hf in range(halves):
        gre_ref[hf] = gs[2 * hf].reshape(l1, nb, LANES)
        gim_ref[hf] = gs[2 * hf + 1].reshape(l1, nb, LANES)


def _fnet_stage2_kernel(gre_ref, gim_ref, twr_ref, twi_ref, f_ref, a_ref, b_ref, bias_ref, o_ref, os):
    halves, kb, l2 = gre_ref.shape[0], gre_ref.shape[1], gre_ref.shape[2]
    cols = []
    for i in range(kb):
        tr, ti = twr_ref[i], twi_ref[i]
        for hf in range(halves):
            gr, gi = gre_ref[hf, i], gim_ref[hf, i]
            cols.append(jnp.concatenate([gr * tr - gi * ti, gr * ti + gi * tr], axis=0))
    y = _dot(f_ref[...], jnp.concatenate(cols, axis=1).astype(BF16))
    y_re = jnp.concatenate([y[:l2, i * FNET_WIDTH:(i + 1) * FNET_WIDTH] for i in range(kb)], axis=0)
    y_im = jnp.concatenate([y[l2:, i * FNET_WIDTH:(i + 1) * FNET_WIDTH] for i in range(kb)], axis=0)
    out = _dot(y_re.astype(BF16), a_ref[...]) + _dot(y_im.astype(BF16), b_ref[...]) + bias_ref[...]
    for i in range(kb):
        for hf in range(halves):
            os[hf, pl.ds(i, l2, stride=kb), :] = out[i * l2:(i + 1) * l2, hf * LANES:(hf + 1) * LANES]
    for hf in range(halves):
        o_ref[hf] = os[hf].reshape(l2, kb, LANES)


def _fnet(v, b, a, bm, bias):
    halves, n, _ = v.shape
    l = n // b
    l2 = FNET_L2
    l1 = l // l2
    c1, s1 = _dft_parts(l1, 1.0 / math.sqrt(l1))
    f1h, f1l = _hilo(np.concatenate([c1, -s1], axis=0))
    c2, s2 = _dft_parts(l2, 1.0 / math.sqrt(l2))
    f2 = jnp.asarray(np.block([[c2, s2], [-s2, c2]]), F32).astype(BF16)
    m = (np.arange(l1)[:, None] * np.arange(l2)[None, :]) % l
    ang = 2.0 * np.pi * m / l
    twr = jnp.broadcast_to(jnp.asarray(np.cos(ang), F32)[:, :, None], (l1, l2, LANES))
    twi = jnp.broadcast_to(jnp.asarray(-np.sin(ang), F32)[:, :, None], (l1, l2, LANES))

    nb = FNET_STAGE1_ROWS // l1
    xs = pl.BlockSpec((halves, None, l1, nb, LANES), lambda i, j: (0, i, 0, j, 0))
    g_shape = jax.ShapeDtypeStruct((halves, b, l1, l2, LANES), F32)
    gre, gim = pl.pallas_call(
        _fnet_stage1_kernel,
        out_shape=[g_shape, g_shape],
        grid=(b, l2 // nb),
        in_specs=[xs, _const_spec(f1h.shape), _const_spec(f1l.shape)],
        out_specs=[xs, xs],
        scratch_shapes=[pltpu.VMEM((halves, l1 * nb, LANES), F32), pltpu.VMEM((2 * halves, l1 * nb, LANES), F32)],
        compiler_params=_params(("parallel", "parallel")),
        name="fnet_stage1",
    )(v.reshape(halves, b, l1, l2, LANES), f1h, f1l)

    kb = SUBLANES
    gs = pl.BlockSpec((halves, None, kb, l2, LANES), lambda i, j: (0, i, j, 0, 0))
    ts = pl.BlockSpec((kb, l2, LANES), lambda i, j: (j, 0, 0))
    y = pl.pallas_call(
        _fnet_stage2_kernel,
        out_shape=jax.ShapeDtypeStruct((halves, b, l2, l1, LANES), F32),
        grid=(b, l1 // kb),
        in_specs=[gs, gs, ts, ts, _const_spec(f2.shape),
                  _const_spec(a.shape), _const_spec(bm.shape), _const_spec(bias.shape)],
        out_specs=pl.BlockSpec((halves, None, l2, kb, LANES), lambda i, j: (0, i, 0, j, 0)),
        scratch_shapes=[pltpu.VMEM((halves, l2 * kb, LANES), F32)],
        compiler_params=_params(("parallel", "parallel")),
        name="fnet_stage2",
    )(gre, gim, twr, twi, f2, a, bm, bias)
    return y.reshape(halves, n, LANES)


def _ones_where(mask):
    return jnp.where(mask, 1.0, 0.0).astype(BF16)


def _softplus(x):
    return jnp.maximum(x, 0.0) + jnp.log1p(jnp.exp(-jnp.abs(x)))


def _ssd_dt_selector():
    sel = np.zeros((2, LANES, SSD_WIDTH), np.float32)
    for d in range(2):
        for lane in range(SSD_WIDTH):
            sel[d, 2 * d * SSD_HEADS + lane // SSD_HEAD_DIM, lane] = 1.0
    return jnp.asarray(sel, BF16)


def _ssd_scan_kernel(xf_ref, xb_ref, dtf_ref, dtb_ref, part_ref, sel_ref, yf_ref, yb_ref, state):
    c = pl.program_id(1)
    t_len = SSD_CHUNK
    n_sub = xf_ref.shape[1] // t_len
    pair_w = 2 * SSD_HEAD_DIM
    heads_per_pair = pair_w // SSD_HEAD_DIM
    n_pairs = SSD_WIDTH // pair_w
    pairs_per_group = n_pairs // SSD_GROUPS

    @pl.when(c == 0)
    def _():
        state[...] = jnp.zeros_like(state)

    row = lax.broadcasted_iota(jnp.int32, (t_len, t_len), 0)
    col = lax.broadcasted_iota(jnp.int32, (t_len, t_len), 1)
    lower, upper = _ones_where(row >= col), _ones_where(row <= col)
    first_head = lax.broadcasted_iota(jnp.int32, (1, pair_w), 1) < SSD_HEAD_DIM
    states = [[state[d, q] for q in range(n_pairs)] for d in range(2)]

    for k in range(n_sub):
        at = [k * t_len, (n_sub - 1 - k) * t_len]
        rows = []
        for d, dt_ref in enumerate((dtf_ref, dtb_ref)):
            dt_row = _softplus(dt_ref[0, at[d]:at[d] + t_len, :].T[0:SSD_HEADS] + part_ref[:, d:d + 1])
            rows += [dt_row, dt_row * (-LOG2_E * jnp.exp(part_ref[:, 2 + d:3 + d]))]
        n_rows = len(rows) * SSD_HEADS
        row_form = jnp.concatenate(rows, axis=0)
        col_form = jnp.concatenate([row_form, jnp.zeros((LANES - n_rows, t_len), F32)], axis=0).T
        cum_col = _dot_exact_lhs(lower, col_form)
        cum_row = _dot_exact_rhs(row_form, upper)
        cf_hi = col_form.astype(BF16)
        cf_lo = (col_form - cf_hi.astype(F32)).astype(BF16)

        for d, (x_ref, y_ref) in enumerate(((xf_ref, yf_ref), (xb_ref, yb_ref))):
            da_at = (2 * d + 1) * SSD_HEADS
            if d == 0:
                keep = row >= col
                p_col = cum_col
                p_row = cum_row[da_at:da_at + SSD_HEADS]
            else:
                keep = col >= row
                p_col = col_form - cum_col
                p_row = row_form[da_at:da_at + SSD_HEADS] - cum_row[da_at:da_at + SSD_HEADS]

            xbc = x_ref[0, at[d]:at[d] + t_len, :]
            dt_wide = _dot(cf_hi, sel_ref[d]) + _dot(cf_lo, sel_ref[d])
            for g in range(SSD_GROUPS):
                b_g = xbc[:, SSD_WIDTH + g * SSD_STATE:SSD_WIDTH + (g + 1) * SSD_STATE]
                c_lo = SSD_WIDTH + SSD_BC + g * SSD_STATE
                c_g = xbc[:, c_lo:c_lo + SSD_STATE].astype(BF16)
                bt_g = b_g.T.astype(BF16)
                cb = _dot(c_g, bt_g)
                for q in range(g * pairs_per_group, (g + 1) * pairs_per_group):
                    h0 = q * heads_per_pair

                    def expand(m):
                        return jnp.where(first_head, m[:, da_at + h0:da_at + h0 + 1],
                                         m[:, da_at + h0 + 1:da_at + h0 + 2])

                    p_e = expand(p_col)
                    xdt = xbc[:, q * pair_w:(q + 1) * pair_w] * dt_wide[:, q * pair_w:(q + 1) * pair_w]
                    if d == 0:
                        total = p_e[t_len - 1:t_len]
                        off_scale = jnp.exp2(p_e)
                        w_state = jnp.exp2(total - p_e)
                    else:
                        total = expand(col_form[t_len - 1:t_len]) - p_e[t_len - 1:t_len]
                        off_scale = jnp.exp2(total + p_e)
                        w_state = jnp.exp2(-p_e)
                    st = states[d][q]
                    y = _dot(c_g, st.astype(BF16)) * off_scale
                    xdt16 = xdt.astype(BF16)
                    intra = []
                    for h in range(h0, h0 + heads_per_pair):
                        diff = p_col[:, da_at + h:da_at + h + 1] - p_row[h:h + 1, :]
                        m = cb * jnp.exp2(jnp.where(keep, diff, -jnp.inf))
                        intra.append(_dot(m.astype(BF16), xdt16))
                    y_ref[0, at[d]:at[d] + t_len, q * pair_w:(q + 1) * pair_w] = (
                        y + jnp.where(first_head, intra[0], intra[1]))
                    states[d][q] = st * jnp.exp2(total) + _dot(bt_g, (xdt * w_state).astype(BF16))

    for d in range(2):
        for q in range(n_pairs):
            state[d, q] = states[d][q]


def _ssd_scan(xbc, dt, part):
    b, l, ch = xbc.shape
    t_blk = SSD_CHUNK * SSD_SUB
    nb = l // t_blk
    pairs = SSD_WIDTH // (2 * SSD_HEAD_DIM)
    sel = _ssd_dt_selector()

    def fwd(width):
        return pl.BlockSpec((1, t_blk, width), lambda i, c: (i, c, 0))

    def bwd(width):
        return pl.BlockSpec((1, t_blk, width), lambda i, c: (i, nb - 1 - c, 0))

    return pl.pallas_call(
        _ssd_scan_kernel,
        out_shape=[jax.ShapeDtypeStruct((b, l, SSD_WIDTH), F32)] * 2,
        grid=(b, nb),
        in_specs=[fwd(ch), bwd(ch), fwd(DT_PAD), bwd(DT_PAD), _const_spec(part.shape), _const_spec(sel.shape)],
        out_specs=[fwd(SSD_WIDTH), bwd(SSD_WIDTH)],
        scratch_shapes=[pltpu.VMEM((2, pairs, SSD_STATE, 2 * SSD_HEAD_DIM), F32)],
        compiler_params=_params(("parallel", "arbitrary")),
        name="ssd_scan",
    )(xbc, xbc, dt, dt, part, sel)


def _mix_ffn_kernel(x_ref, s5f_ref, s5b_ref, u_ref, fn_ref, sdf_ref, sdb_ref, xc_ref, z_ref,
                    s5d_ref, wglu_ref, bglu_ref, sdd_ref, sdg_ref, wo_ref,
                    g_ref, wg_ref, wu_ref, wd_ref, gf_ref, o_ref, *, final_norm):
    ya = s5f_ref[...] + s5b_ref[...] + u_ref[...] * s5d_ref[...]
    ga = jax.nn.gelu(ya)
    ya = ga * jax.nn.sigmoid(_dot(ga.astype(BF16), wglu_ref[...]) + bglu_ref[...])
    z = z_ref[...]
    yc = (sdf_ref[...] + sdb_ref[...] + xc_ref[...] * sdd_ref[...]) * (z * jax.nn.sigmoid(z))
    yc = _rmsnorm(yc, sdg_ref[...])
    lo, mid = S5_WIDTH, S5_WIDTH + FNET_WIDTH
    x = (x_ref[...] + _dot(ya.astype(BF16), wo_ref[0:lo, :])
         + _dot(jnp.concatenate([fn_ref[hf] for hf in range(fn_ref.shape[0])], axis=1).astype(BF16), wo_ref[lo:mid, :])
         + _dot(yc.astype(BF16), wo_ref[mid:, :]))
    h = _rmsnorm(x, g_ref[...]).astype(BF16)
    gate = _dot(h, wg_ref[...])
    act = (gate * jax.nn.sigmoid(gate) * _dot(h, wu_ref[...])).astype(BF16)
    y = x + _dot(act, wd_ref[...])
    o_ref[...] = _rmsnorm(y, gf_ref[...]) if final_norm else y


def _mix_ffn(x, s5f, s5b, u, fn, sdf, sdb, xbc_c, z, consts, final_norm):
    n = x.shape[0]

    def rows(width):
        return pl.BlockSpec((FFN_TILE, width), lambda i: (i, 0))

    return pl.pallas_call(
        functools.partial(_mix_ffn_kernel, final_norm=final_norm),
        out_shape=jax.ShapeDtypeStruct(x.shape, F32),
        grid=(n // FFN_TILE,),
        in_specs=[rows(D_MODEL), rows(S5_WIDTH), rows(S5_WIDTH), rows(S5_WIDTH),
                  pl.BlockSpec((fn.shape[0], FFN_TILE, LANES), lambda i: (0, i, 0)),
                  rows(SSD_WIDTH), rows(SSD_WIDTH), rows(SSD_WIDTH), rows(SSD_WIDTH)]
                 + [_const_spec(a.shape) for a in consts],
        out_specs=rows(D_MODEL),
        compiler_params=_params(("parallel",)),
        name="mix_ffn",
    )(x, s5f, s5b, u, fn, sdf, sdb, xbc_c, z, *consts)


def _row(v):
    return v.reshape(1, -1).astype(F32)


def _prepare_layer(i, p):
    lay = {}
    lay['norm_mix_g'] = _row(p['norm_mix_g'][i])
    lay['w_in'] = jnp.pad(p['w_in'][i], ((0, 0), (0, IN_PROJ_PAD - p['w_in'].shape[-1]))).astype(BF16)
    lay['s5'] = _s5_prepare(
        p['s5_b_re'][i], p['s5_b_im'][i],
        [(p['s5_lam_re_f'][i], p['s5_lam_im_f'][i], p['s5_log_step_f'][i], p['s5_c_re_f'][i], p['s5_c_im_f'][i]),
         (p['s5_lam_re_b'][i], p['s5_lam_im_b'][i], p['s5_log_step_b'][i], p['s5_c_re_b'][i], p['s5_c_im_b'][i])])
    lay['s5_d'] = _row(p['s5_d'][i])
    lay['s5_w_glu'] = p['s5_w_glu'][i].astype(BF16)
    lay['s5_b_glu'] = _row(p['s5_b_glu'][i])
    lay['fnet_ab'] = _fnet_fold(p['fnet_w'][i])
    lay['fnet_b'] = _row(p['fnet_b'][i])
    lay['conv_w'] = p['ssd_conv_w'][i].astype(F32)
    lay['conv_b'] = _row(p['ssd_conv_b'][i])
    par = jnp.stack([p['ssd_dt_bias_f'][i], p['ssd_dt_bias_b'][i], p['ssd_a_log_f'][i], p['ssd_a_log_b'][i]]).astype(F32)
    lay['ssd_part'] = par.T
    lay['ssd_d'] = _row(jnp.repeat(p['ssd_d'][i], SSD_HEAD_DIM))
    lay['ssd_norm_g'] = _row(p['ssd_norm_g'][i])
    lay['w_out'] = p['w_out'][i].astype(BF16)
    lay['norm_ffn_g'] = _row(p['norm_ffn_g'][i])
    lay['w_gate'] = p['w_gate'][i].astype(BF16)
    lay['w_up'] = p['w_up'][i].astype(BF16)
    lay['w_down'] = p['w_down'][i].astype(BF16)
    return lay


def _trunk(x3, layers, final_g):
    b, l, dm = x3.shape
    x = x3.reshape(b * l, dm)
    for i, lay in enumerate(layers):
        u, v, z, xbc_c, dt = _norm_inproj(x, lay['norm_mix_g'], lay['w_in'], lay['conv_w'], lay['conv_b'], l)
        s5f, s5b = _s5_scan(u.reshape(b, l, -1), *lay['s5'])
        fn = _fnet(v, b, *lay['fnet_ab'], lay['fnet_b'])
        sdf, sdb = _ssd_scan(xbc_c.reshape(b, l, -1), dt.reshape(b, l, -1), lay['ssd_part'])
        n = b * l
        consts = (lay['s5_d'], lay['s5_w_glu'], lay['s5_b_glu'], lay['ssd_d'], lay['ssd_norm_g'], lay['w_out'],
                  lay['norm_ffn_g'], lay['w_gate'], lay['w_up'], lay['w_down'], final_g)
        x = _mix_ffn(x, s5f.reshape(n, -1), s5b.reshape(n, -1), u, fn,
                     sdf.reshape(n, -1), sdb.reshape(n, -1), xbc_c, z, consts,
                     final_norm=(i == len(layers) - 1))
    return x.reshape(b, l, dm)


def kernel(x_prompt, x_sample, norm_mix_g, w_in, s5_b_re, s5_b_im, s5_lam_re_f, s5_lam_im_f, s5_log_step_f, s5_c_re_f, s5_c_im_f, s5_lam_re_b, s5_lam_im_b, s5_log_step_b, s5_c_re_b, s5_c_im_b, s5_d, s5_w_glu, s5_b_glu, fnet_w, fnet_b, ssd_conv_w, ssd_conv_b, ssd_a_log_f, ssd_dt_bias_f, ssd_a_log_b, ssd_dt_bias_b, ssd_d, ssd_norm_g, w_out, norm_ffn_g, w_gate, w_up, w_down, final_norm_g):
    p = dict(norm_mix_g=norm_mix_g, w_in=w_in, s5_b_re=s5_b_re, s5_b_im=s5_b_im,
             s5_lam_re_f=s5_lam_re_f, s5_lam_im_f=s5_lam_im_f, s5_log_step_f=s5_log_step_f,
             s5_c_re_f=s5_c_re_f, s5_c_im_f=s5_c_im_f,
             s5_lam_re_b=s5_lam_re_b, s5_lam_im_b=s5_lam_im_b, s5_log_step_b=s5_log_step_b,
             s5_c_re_b=s5_c_re_b, s5_c_im_b=s5_c_im_b,
             s5_d=s5_d, s5_w_glu=s5_w_glu, s5_b_glu=s5_b_glu, fnet_w=fnet_w, fnet_b=fnet_b,
             ssd_conv_w=ssd_conv_w, ssd_conv_b=ssd_conv_b, ssd_a_log_f=ssd_a_log_f,
             ssd_dt_bias_f=ssd_dt_bias_f, ssd_a_log_b=ssd_a_log_b, ssd_dt_bias_b=ssd_dt_bias_b,
             ssd_d=ssd_d, ssd_norm_g=ssd_norm_g, w_out=w_out, norm_ffn_g=norm_ffn_g,
             w_gate=w_gate, w_up=w_up, w_down=w_down)
    layers = [_prepare_layer(i, p) for i in range(norm_mix_g.shape[0])]
    final_g = _row(final_norm_g)
    return (_trunk(x_prompt, layers, final_g), _trunk(x_sample, layers, final_g))
```

```python
import functools
import math

import numpy as np
import jax
import jax.numpy as jnp
from jax import lax
from jax.experimental import pallas as pl
from jax.experimental.pallas import tpu as pltpu

F32 = jnp.float32
BF16 = jnp.bfloat16

D_MODEL = 1024
S5_WIDTH = 256
S5_GROUP = 16
S5_GROUPS = 16
S5_STATE = 64
S5_NSTATE = S5_GROUPS * S5_STATE
FNET_WIDTH = 256
FNET_GROUPS = 4
FNET_GROUP = 64
SSD_WIDTH = 512
SSD_HEAD_DIM = 64
SSD_HEADS = 8
SSD_GROUPS = 2
SSD_STATE = 128
SSD_CONV = 5
SSD_BC = SSD_GROUPS * SSD_STATE
SSD_CONV_DIM = SSD_WIDTH + 2 * SSD_BC
DT_PAD = 128
IN_PROJ_PAD = S5_WIDTH + FNET_WIDTH + SSD_WIDTH + SSD_CONV_DIM + DT_PAD
D_FF = 2816
EPS = 1e-6
LOG2_E = math.log2(math.e)

LANES = 128
SUBLANES = 8
VMEM_LIMIT = 56 * 1024 * 1024

TOKEN_TILE = 512
CONV_BLOCK = 256
FFN_TILE = 512
S5_CHUNK = 256
S5_SEQS = 2
SSD_CHUNK = 128
SSD_SUB = 4
FNET_L2 = 128
FNET_STAGE1_ROWS = 512


def _rmsnorm(x, g):
    return x * lax.rsqrt(jnp.mean(x * x, axis=-1, keepdims=True) + EPS) * g


def _split3(v):
    hi = v.astype(BF16)
    r = v - hi.astype(F32)
    mid = r.astype(BF16)
    lo = (r - mid.astype(F32)).astype(BF16)
    return hi, mid, lo


def _dot(a, b):
    return jnp.dot(a, b, preferred_element_type=F32)


def _dot_exact_lhs(m, v):
    hi, mid, lo = _split3(v)
    return _dot(m, hi) + _dot(m, mid) + _dot(m, lo)


def _dot_exact_rhs(v, m):
    hi, mid, lo = _split3(v)
    return _dot(hi, m) + _dot(mid, m) + _dot(lo, m)


def _dot_hilo(m_hi, m_lo, v):
    v_hi = v.astype(BF16)
    v_lo = (v - v_hi.astype(F32)).astype(BF16)
    return _dot(m_hi, v_hi) + _dot(m_hi, v_lo) + _dot(m_lo, v_hi)


def _const_spec(shape):
    nd = len(shape)
    return pl.BlockSpec(shape, lambda *_: (0,) * nd, pipeline_mode=pl.Buffered(1))


def _params(semantics):
    return pltpu.CompilerParams(dimension_semantics=semantics, vmem_limit_bytes=VMEM_LIMIT)


_INPROJ_SPLITS = (0, S5_WIDTH, S5_WIDTH + FNET_WIDTH, S5_WIDTH + FNET_WIDTH + SSD_WIDTH,
                  S5_WIDTH + FNET_WIDTH + SSD_WIDTH + SSD_CONV_DIM, IN_PROJ_PAD)


def _norm_inproj_kernel(prev_ref, x_ref, next_ref, g_ref, w_ref, cw_ref, cb_ref,
                        u_ref, v_ref, z_ref, xbc_ref, dt_ref, *, tiles_per_seq):
    i = pl.program_id(0)
    t_len = x_ref.shape[0]
    pad = SSD_CONV // 2
    rows = t_len + 2 * SUBLANES
    y_ext = _rmsnorm(jnp.concatenate([prev_ref[...], x_ref[...], next_ref[...]], axis=0), g_ref[...])
    h = y_ext[SUBLANES:SUBLANES + t_len].astype(BF16)
    h_ext = y_ext.astype(BF16)
    lo_xbc = _INPROJ_SPLITS[3]
    first = i % tiles_per_seq == 0
    last = i % tiles_per_seq == tiles_per_seq - 1

    def other_dot(which):
        lo, hi = _INPROJ_SPLITS[which], _INPROJ_SPLITS[which + 1]
        res = _dot(h, w_ref[:, lo:hi])
        if which == 1:
            for hf in range(FNET_WIDTH // LANES):
                v_ref[hf] = res[:, hf * LANES:(hf + 1) * LANES]
        else:
            (u_ref, None, z_ref, None, dt_ref)[which][...] = res

    others = [0, 1, 2, 4]
    for j in range(SSD_CONV_DIM // CONV_BLOCK):
        cols = slice(j * CONV_BLOCK, (j + 1) * CONV_BLOCK)
        d = _dot(h_ext, w_ref[:, lo_xbc + j * CONV_BLOCK:lo_xbc + (j + 1) * CONV_BLOCK])
        xe = jnp.concatenate([jnp.where(first, 0.0, d[0:SUBLANES]), d[SUBLANES:SUBLANES + t_len],
                              jnp.where(last, 0.0, d[SUBLANES + t_len:])], axis=0)
        if others:
            other_dot(others.pop(0))
        acc = jnp.broadcast_to(cb_ref[:, cols], (t_len, CONV_BLOCK))
        for k in range(SSD_CONV):
            tap = xe if k == pad else pltpu.roll(xe, (pad - k) % rows, axis=0)
            acc = acc + cw_ref[k:k + 1, cols] * tap[SUBLANES:SUBLANES + t_len]
        xbc_ref[:, cols] = acc * jax.nn.sigmoid(acc)
    while others:
        other_dot(others.pop(0))


def _norm_inproj(x, g, w, conv_w, conv_b, seq_len):
    n = x.shape[0]
    widths = [hi - lo for lo, hi in zip(_INPROJ_SPLITS[:-1], _INPROJ_SPLITS[1:])]
    per = TOKEN_TILE // SUBLANES
    last = n // SUBLANES - 1
    halves = FNET_WIDTH // LANES
    out_shape = [jax.ShapeDtypeStruct((n, wd), F32) for wd in widths]
    out_specs = [pl.BlockSpec((TOKEN_TILE, wd), lambda i: (i, 0)) for wd in widths]
    out_shape[1] = jax.ShapeDtypeStruct((halves, n, LANES), F32)
    out_specs[1] = pl.BlockSpec((halves, TOKEN_TILE, LANES), lambda i: (0, i, 0))
    return pl.pallas_call(
        functools.partial(_norm_inproj_kernel, tiles_per_seq=seq_len // TOKEN_TILE),
        out_shape=out_shape,
        grid=(n // TOKEN_TILE,),
        in_specs=[pl.BlockSpec((SUBLANES, D_MODEL), lambda i: (jnp.maximum(i * per - 1, 0), 0)),
                  pl.BlockSpec((TOKEN_TILE, D_MODEL), lambda i: (i, 0)),
                  pl.BlockSpec((SUBLANES, D_MODEL), lambda i: (jnp.minimum((i + 1) * per, last), 0)),
                  _const_spec((1, D_MODEL)), _const_spec((D_MODEL, IN_PROJ_PAD)),
                  _const_spec(conv_w.shape), _const_spec(conv_b.shape)],
        out_specs=out_specs,
        compiler_params=_params(("parallel",)),
        name="norm_inproj",
    )(x, x, x, g, w, conv_w, conv_b)


def _s5_scan_kernel(uf_ref, ub_ref, bmat_ref, cmat_ref, lam_ref, yf_ref, yb_ref, bu0, bu1, h0, h1, carry):
    s = pl.program_id(1)
    nseq, t_len = uf_ref.shape[0], uf_ref.shape[1] // 2
    chains = [(i, d) for i in range(nseq) for d in range(2)]
    lam = [lam_ref[k] for k in range(4)]

    @pl.when(s == 0)
    def _():
        carry[...] = jnp.zeros_like(carry)
        bu1[...] = jnp.zeros_like(bu1)
        h0[...] = jnp.zeros_like(h0)

    def rows_of(half, d):
        lo = half * t_len if d == 0 else (1 - half) * t_len
        return pl.ds(lo, t_len)

    def project(half, bu, i, d):
        u16 = (uf_ref if d == 0 else ub_ref)[i, rows_of(half, d), :].astype(BF16)
        for part in range(2):
            col = (2 * d + part) * S5_NSTATE
            prod = _dot(u16, bmat_ref[:, col:col + S5_NSTATE])
            k = (i * 2 + d) * 2 + part
            for j in range(SUBLANES):
                bu[k, pl.ds(j, t_len, stride=SUBLANES), :] = prod[:, j * LANES:(j + 1) * LANES]

    def scan(bu, hb, hs, t_lo, t_hi):
        for t in range(t_lo, t_hi):
            for n, (i, d) in enumerate(chains):
                kre = (i * 2 + d) * 2
                row = pl.ds((t if d == 0 else t_len - 1 - t) * SUBLANES, SUBLANES)
                a_re, a_im = lam[2 * d], lam[2 * d + 1]
                h_re, h_im = hs[2 * n], hs[2 * n + 1]
                hs[2 * n] = a_re * h_re - a_im * h_im + bu[kre, row, :]
                hs[2 * n + 1] = a_re * h_im + a_im * h_re + bu[kre + 1, row, :]
                hb[kre, row, :] = hs[2 * n]
                hb[kre + 1, row, :] = hs[2 * n + 1]

    def read_out(half, hb, i, d):
        kre = (i * 2 + d) * 2
        h = jnp.concatenate(
            [hb[kre + part, pl.ds(j, t_len, stride=SUBLANES), :] for part in range(2) for j in range(SUBLANES)],
            axis=1).astype(BF16)
        (yf_ref if d == 0 else yb_ref)[i, rows_of(half, d), :] = _dot(h, cmat_ref[d])

    hs = [carry[k] for k in range(4 * nseq)]
    per = t_len // len(chains)
    for half, (bu_w, bu_r, h_w, h_r) in enumerate(((bu0, bu1, h1, h0), (bu1, bu0, h0, h1))):
        for n, (i, d) in enumerate(chains):
            project(half, bu_w, i, d)
            scan(bu_r, h_w, hs, n * per, (n + 1) * per)
            read_out(half, h_r, i, d)
    for k in range(4 * nseq):
        carry[k] = hs[k]


def _s5_scan(u, bmat, cmat, lam):
    b, l, _ = u.shape
    nb = l // (2 * S5_CHUNK)
    blk = (S5_SEQS, 2 * S5_CHUNK, S5_WIDTH)
    in_f = pl.BlockSpec(blk, lambda i, s: (i, jnp.minimum(s, nb - 1), 0))
    in_b = pl.BlockSpec(blk, lambda i, s: (i, nb - 1 - jnp.minimum(s, nb - 1), 0))
    out_f = pl.BlockSpec(blk, lambda i, s: (i, jnp.maximum(s - 1, 0), 0))
    out_b = pl.BlockSpec(blk, lambda i, s: (i, nb - 1 - jnp.maximum(s - 1, 0), 0))
    buf = pltpu.VMEM((S5_SEQS * 4, S5_CHUNK * SUBLANES, LANES), F32)
    return pl.pallas_call(
        _s5_scan_kernel,
        out_shape=[jax.ShapeDtypeStruct(u.shape, F32)] * 2,
        grid=(b // S5_SEQS, nb + 1),
        in_specs=[in_f, in_b, _const_spec(bmat.shape), _const_spec(cmat.shape), _const_spec(lam.shape)],
        out_specs=[out_f, out_b],
        scratch_shapes=[buf, buf, buf, buf, pltpu.VMEM((S5_SEQS * 4, SUBLANES, LANES), F32)],
        compiler_params=_params(("parallel", "arbitrary")),
        name="s5_scan",
    )(u, u, bmat, cmat, lam)


def _s5_prepare(b_re, b_im, dirs):
    g, p, c = S5_GROUPS, S5_STATE, S5_GROUP
    eye = jnp.eye(g, dtype=F32)
    bcols, cmats, lams = [], [], []
    for lam_re, lam_im, log_step, c_re, c_im in dirs:
        step = jnp.exp(log_step)[:, None]
        mag = jnp.exp(lam_re * step)
        lb_re, lb_im = mag * jnp.cos(lam_im * step), mag * jnp.sin(lam_im * step)
        den = lam_re * lam_re + lam_im * lam_im
        q_re = ((lb_re - 1.0) * lam_re + lb_im * lam_im) / den
        q_im = (lb_im * lam_re - (lb_re - 1.0) * lam_im) / den
        bb_re = q_re[..., None] * b_re - q_im[..., None] * b_im
        bb_im = q_re[..., None] * b_im + q_im[..., None] * b_re
        for bb in (bb_re, bb_im):
            bcols.append(jnp.einsum('gpc,gh->gchp', bb, eye).reshape(g * c, g * p))
        cm = [jnp.einsum('gcp,gh->gphc', cc, eye).reshape(g * p, g * c) for cc in (c_re, -c_im)]
        cmats.append(jnp.concatenate(cm, axis=0))
        lams += [lb_re.reshape(SUBLANES, LANES), lb_im.reshape(SUBLANES, LANES)]
    return (jnp.concatenate(bcols, axis=1).astype(BF16), jnp.stack(cmats).astype(BF16), jnp.stack(lams))


def _dft_parts(n, scale):
    k = np.arange(n)
    ang = 2.0 * np.pi * ((k[:, None] * k[None, :]) % n) / n
    return np.cos(ang) * scale, np.sin(ang) * scale


def _hilo(m):
    m = np.asarray(m, np.float32)
    hi = jnp.asarray(m, F32).astype(BF16)
    lo = (jnp.asarray(m, F32) - hi.astype(F32)).astype(BF16)
    return hi, lo


def _fnet_fold_kernel(ch_ref, cl_ref, sh_ref, sl_ref, w_ref, a_ref, b_ref):
    w = w_ref[...]
    a_ref[...] = _dot_hilo(ch_ref[...], cl_ref[...], w).astype(BF16)
    b_ref[...] = _dot_hilo(sh_ref[...], sl_ref[...], w).astype(BF16)


def _fnet_fold(w):
    c64, s64 = _dft_parts(FNET_GROUP, 1.0 / math.sqrt(FNET_GROUP))
    eye = np.eye(FNET_GROUPS)
    cbd, sbd = np.kron(eye, c64), np.kron(eye, s64)
    wbd = jnp.einsum('gcd,gh->gchd', w, jnp.eye(FNET_GROUPS, dtype=F32)).reshape(FNET_WIDTH, FNET_WIDTH)
    shape = (FNET_WIDTH, FNET_WIDTH)
    return pl.pallas_call(
        _fnet_fold_kernel,
        out_shape=[jax.ShapeDtypeStruct(shape, BF16)] * 2,
        name="fnet_fold",
    )(*_hilo(cbd), *_hilo(sbd), wbd)


def _fnet_stage1_kernel(x_ref, fh_ref, fl_ref, gre_ref, gim_ref, xs, gs):
    halves, l1, nb = x_ref.shape[0], x_ref.shape[1], x_ref.shape[2]
    for hf in range(halves):
        xs[hf] = x_ref[hf].reshape(l1 * nb, LANES)
    order = [(hf, s) for hf in range(halves) for s in range(nb)]
    g = _dot_hilo(fh_ref[...], fl_ref[...],
                  jnp.concatenate([xs[hf, pl.ds(s, l1, stride=nb), :] for hf, s in order], axis=1))
    for idx, (hf, s) in enumerate(order):
        gs[2 * hf, pl.ds(s, l1, stride=nb), :] = g[:l1, idx * LANES:(idx + 1) * LANES]
        gs[2 * hf + 1, pl.ds(s, l1, stride=nb), :] = g[l1:, idx * LANES:(idx + 1) * LANES]
    for hf in range(halves):
        gre_ref[hf] = gs[2 * hf].reshape(l1, nb, LANES)
        gim_ref[hf] = gs[2 * hf + 1].reshape(l1, nb, LANES)


def _fnet_stage2_kernel(gre_ref, gim_ref, twr_ref, twi_ref, f_ref, a_ref, b_ref, bias_ref, o_ref, os):
    halves, kb, l2 = gre_ref.shape[0], gre_ref.shape[1], gre_ref.shape[2]
    cols = []
    for i in range(kb):
        tr, ti = twr_ref[i], twi_ref[i]
        for hf in range(halves):
            gr, gi = gre_ref[hf, i], gim_ref[hf, i]
            cols.append(jnp.concatenate([gr * tr - gi * ti, gr * ti + gi * tr], axis=0))
    y = _dot(f_ref[...], jnp.concatenate(cols, axis=1).astype(BF16))
    y_re = jnp.concatenate([y[:l2, i * FNET_WIDTH:(i + 1) * FNET_WIDTH] for i in range(kb)], axis=0)
    y_im = jnp.concatenate([y[l2:, i * FNET_WIDTH:(i + 1) * FNET_WIDTH] for i in range(kb)], axis=0)
    out = _dot(y_re.astype(BF16), a_ref[...]) + _dot(y_im.astype(BF16), b_ref[...]) + bias_ref[...]
    for i in range(kb):
        for hf in range(halves):
            os[hf, pl.ds(i, l2, stride=kb), :] = out[i * l2:(i + 1) * l2, hf * LANES:(hf + 1) * LANES]
    for hf in range(halves):
        o_ref[hf] = os[hf].reshape(l2, kb, LANES)


def _fnet(v, b, a, bm, bias):
    halves, n, _ = v.shape
    l = n // b
    l2 = FNET_L2
    l1 = l // l2
    c1, s1 = _dft_parts(l1, 1.0 / math.sqrt(l1))
    f1h, f1l = _hilo(np.concatenate([c1, -s1], axis=0))
    c2, s2 = _dft_parts(l2, 1.0 / math.sqrt(l2))
    f2 = jnp.asarray(np.block([[c2, s2], [-s2, c2]]), F32).astype(BF16)
    m = (np.arange(l1)[:, None] * np.arange(l2)[None, :]) % l
    ang = 2.0 * np.pi * m / l
    twr = jnp.broadcast_to(jnp.asarray(np.cos(ang), F32)[:, :, None], (l1, l2, LANES))
    twi = jnp.broadcast_to(jnp.asarray(-np.sin(ang), F32)[:, :, None], (l1, l2, LANES))

    nb = FNET_STAGE1_ROWS // l1
    xs = pl.BlockSpec((halves, None, l1, nb, LANES), lambda i, j: (0, i, 0, j, 0))
    g_shape = jax.ShapeDtypeStruct((halves, b, l1, l2, LANES), F32)
    gre, gim = pl.pallas_call(
        _fnet_stage1_kernel,
        out_shape=[g_shape, g_shape],
        grid=(b, l2 // nb),
        in_specs=[xs, _const_spec(f1h.shape), _const_spec(f1l.shape)],
        out_specs=[xs, xs],
        scratch_shapes=[pltpu.VMEM((halves, l1 * nb, LANES), F32), pltpu.VMEM((2 * halves, l1 * nb, LANES), F32)],
        compiler_params=_params(("parallel", "parallel")),
        name="fnet_stage1",
    )(v.reshape(halves, b, l1, l2, LANES), f1h, f1l)

    kb = SUBLANES
    gs = pl.BlockSpec((halves, None, kb, l2, LANES), lambda i, j: (0, i, j, 0, 0))
    ts = pl.BlockSpec((kb, l2, LANES), lambda i, j: (j, 0, 0))
    y = pl.pallas_call(
        _fnet_stage2_kernel,
        out_shape=jax.ShapeDtypeStruct((halves, b, l2, l1, LANES), F32),
        grid=(b, l1 // kb),
        in_specs=[gs, gs, ts, ts, _const_spec(f2.shape),
                  _const_spec(a.shape), _const_spec(bm.shape), _const_spec(bias.shape)],
        out_specs=pl.BlockSpec((halves, None, l2, kb, LANES), lambda i, j: (0, i, 0, j, 0)),
        scratch_shapes=[pltpu.VMEM((halves, l2 * kb, LANES), F32)],
        compiler_params=_params(("parallel", "parallel")),
        name="fnet_stage2",
    )(gre, gim, twr, twi, f2, a, bm, bias)
    return y.reshape(halves, n, LANES)


def _ones_where(mask):
    return jnp.where(mask, 1.0, 0.0).astype(BF16)


def _softplus(x):
    return jnp.maximum(x, 0.0) + jnp.log1p(jnp.exp(-jnp.abs(x)))


def _ssd_dt_selector():
    sel = np.zeros((2, LANES, SSD_WIDTH), np.float32)
    for d in range(2):
        for lane in range(SSD_WIDTH):
            sel[d, 2 * d * SSD_HEADS + lane // SSD_HEAD_DIM, lane] = 1.0
    return jnp.asarray(sel, BF16)


def _ssd_scan_kernel(xf_ref, xb_ref, dtf_ref, dtb_ref, part_ref, sel_ref, yf_ref, yb_ref, state):
    c = pl.program_id(1)
    t_len = SSD_CHUNK
    n_sub = xf_ref.shape[1] // t_len
    pair_w = 2 * SSD_HEAD_DIM
    heads_per_pair = pair_w // SSD_HEAD_DIM
    n_pairs = SSD_WIDTH // pair_w
    pairs_per_group = n_pairs // SSD_GROUPS

    @pl.when(c == 0)
    def _():
        state[...] = jnp.zeros_like(state)

    row = lax.broadcasted_iota(jnp.int32, (t_len, t_len), 0)
    col = lax.broadcasted_iota(jnp.int32, (t_len, t_len), 1)
    lower, upper = _ones_where(row >= col), _ones_where(row <= col)
    first_head = lax.broadcasted_iota(jnp.int32, (1, pair_w), 1) < SSD_HEAD_DIM
    states = [[state[d, q] for q in range(n_pairs)] for d in range(2)]

    for k in range(n_sub):
        at = [k * t_len, (n_sub - 1 - k) * t_len]
        rows = []
        for d, dt_ref in enumerate((dtf_ref, dtb_ref)):
            dt_row = _softplus(dt_ref[0, at[d]:at[d] + t_len, :].T[0:SSD_HEADS] + part_ref[:, d:d + 1])
            rows += [dt_row, dt_row * (-LOG2_E * jnp.exp(part_ref[:, 2 + d:3 + d]))]
        n_rows = len(rows) * SSD_HEADS
        row_form = jnp.concatenate(rows, axis=0)
        col_form = jnp.concatenate([row_form, jnp.zeros((LANES - n_rows, t_len), F32)], axis=0).T
        cum_col = _dot_exact_lhs(lower, col_form)
        cum_row = _dot_exact_rhs(row_form, upper)
        cf_hi = col_form.astype(BF16)
        cf_lo = (col_form - cf_hi.astype(F32)).astype(BF16)

        for d, (x_ref, y_ref) in enumerate(((xf_ref, yf_ref), (xb_ref, yb_ref))):
            da_at = (2 * d + 1) * SSD_HEADS
            if d == 0:
                keep = row >= col
                p_col = cum_col
                p_row = cum_row[da_at:da_at + SSD_HEADS]
            else:
                keep = col >= row
                p_col = col_form - cum_col
                p_row = row_form[da_at:da_at + SSD_HEADS] - cum_row[da_at:da_at + SSD_HEADS]

            xbc = x_ref[0, at[d]:at[d] + t_len, :]
            dt_wide = _dot(cf_hi, sel_ref[d]) + _dot(cf_lo, sel_ref[d])
            for g in range(SSD_GROUPS):
                b_g = xbc[:, SSD_WIDTH + g * SSD_STATE:SSD_WIDTH + (g + 1) * SSD_STATE]
                c_lo = SSD_WIDTH + SSD_BC + g * SSD_STATE
                c_g = xbc[:, c_lo:c_lo + SSD_STATE].astype(BF16)
                bt_g = b_g.T.astype(BF16)
                cb = _dot(c_g, bt_g)
                for q in range(g * pairs_per_group, (g + 1) * pairs_per_group):
                    h0 = q * heads_per_pair

                    def expand(m):
                        return jnp.where(first_head, m[:, da_at + h0:da_at + h0 + 1],
                                         m[:, da_at + h0 + 1:da_at + h0 + 2])

                    p_e = expand(p_col)
                    xdt = xbc[:, q * pair_w:(q + 1) * pair_w] * dt_wide[:, q * pair_w:(q + 1) * pair_w]
                    if d == 0:
                        total = p_e[t_len - 1:t_len]
                        off_scale = jnp.exp2(p_e)
                        w_state = jnp.exp2(total - p_e)
                    else:
                        total = expand(col_form[t_len - 1:t_len]) - p_e[t_len - 1:t_len]
                        off_scale = jnp.exp2(total + p_e)
                        w_state = jnp.exp2(-p_e)
                    st = states[d][q]
                    y = _dot(c_g, st.astype(BF16)) * off_scale
                    xdt16 = xdt.astype(BF16)
                    intra = []
                    for h in range(h0, h0 + heads_per_pair):
                        diff = p_col[:, da_at + h:da_at + h + 1] - p_row[h:h + 1, :]
                        m = cb * jnp.exp2(jnp.where(keep, diff, -jnp.inf))
                        intra.append(_dot(m.astype(BF16), xdt16))
                    y_ref[0, at[d]:at[d] + t_len, q * pair_w:(q + 1) * pair_w] = (
                        y + jnp.where(first_head, intra[0], intra[1]))
                    states[d][q] = st * jnp.exp2(total) + _dot(bt_g, (xdt * w_state).astype(BF16))

    for d in range(2):
        for q in range(n_pairs):
            state[d, q] = states[d][q]


def _ssd_scan(xbc, dt, part):
    b, l, ch = xbc.shape
    t_blk = SSD_CHUNK * SSD_SUB
    nb = l // t_blk
    pairs = SSD_WIDTH // (2 * SSD_HEAD_DIM)
    sel = _ssd_dt_selector()

    def fwd(width):
        return pl.BlockSpec((1, t_blk, width), lambda i, c: (i, c, 0))

    def bwd(width):
        return pl.BlockSpec((1, t_blk, width), lambda i, c: (i, nb - 1 - c, 0))

    return pl.pallas_call(
        _ssd_scan_kernel,
        out_shape=[jax.ShapeDtypeStruct((b, l, SSD_WIDTH), F32)] * 2,
        grid=(b, nb),
        in_specs=[fwd(ch), bwd(ch), fwd(DT_PAD), bwd(DT_PAD), _const_spec(part.shape), _const_spec(sel.shape)],
        out_specs=[fwd(SSD_WIDTH), bwd(SSD_WIDTH)],
        scratch_shapes=[pltpu.VMEM((2, pairs, SSD_STATE, 2 * SSD_HEAD_DIM), F32)],
        compiler_params=_params(("parallel", "arbitrary")),
        name="ssd_scan",
    )(xbc, xbc, dt, dt, part, sel)


def _mix_ffn_kernel(x_ref, s5f_ref, s5b_ref, u_ref, fn_ref, sdf_ref, sdb_ref, xc_ref, z_ref,
                    s5d_ref, wglu_ref, bglu_ref, sdd_ref, sdg_ref, wo_ref,
                    g_ref, wg_ref, wu_ref, wd_ref, gf_ref, o_ref, *, final_norm):
    ya = s5f_ref[...] + s5b_ref[...] + u_ref[...] * s5d_ref[...]
    ga = jax.nn.gelu(ya)
    ya = ga * jax.nn.sigmoid(_dot(ga.astype(BF16), wglu_ref[...]) + bglu_ref[...])
    z = z_ref[...]
    yc = (sdf_ref[...] + sdb_ref[...] + xc_ref[...] * sdd_ref[...]) * (z * jax.nn.sigmoid(z))
    yc = _rmsnorm(yc, sdg_ref[...])
    lo, mid = S5_WIDTH, S5_WIDTH + FNET_WIDTH
    x = (x_ref[...] + _dot(ya.astype(BF16), wo_ref[0:lo, :])
         + _dot(jnp.concatenate([fn_ref[hf] for hf in range(fn_ref.shape[0])], axis=1).astype(BF16), wo_ref[lo:mid, :])
         + _dot(yc.astype(BF16), wo_ref[mid:, :]))
    h = _rmsnorm(x, g_ref[...]).astype(BF16)
    gate = _dot(h, wg_ref[...])
    act = (gate * jax.nn.sigmoid(gate) * _dot(h, wu_ref[...])).astype(BF16)
    y = x + _dot(act, wd_ref[...])
    o_ref[...] = _rmsnorm(y, gf_ref[...]) if final_norm else y


def _mix_ffn(x, s5f, s5b, u, fn, sdf, sdb, xbc_c, z, consts, final_norm):
    n = x.shape[0]

    def rows(width):
        return pl.BlockSpec((FFN_TILE, width), lambda i: (i, 0))

    return pl.pallas_call(
        functools.partial(_mix_ffn_kernel, final_norm=final_norm),
        out_shape=jax.ShapeDtypeStruct(x.shape, F32),
        grid=(n // FFN_TILE,),
        in_specs=[rows(D_MODEL), rows(S5_WIDTH), rows(S5_WIDTH), rows(S5_WIDTH),
                  pl.BlockSpec((fn.shape[0], FFN_TILE, LANES), lambda i: (0, i, 0)),
                  rows(SSD_WIDTH), rows(SSD_WIDTH), rows(SSD_WIDTH), rows(SSD_WIDTH)]
                 + [_const_spec(a.shape) for a in consts],
        out_specs=rows(D_MODEL),
        compiler_params=_params(("parallel",)),
        name="mix_ffn",
    )(x, s5f, s5b, u, fn, sdf, sdb, xbc_c, z, *consts)


def _row(v):
    return v.reshape(1, -1).astype(F32)


def _prepare_layer(i, p):
    lay = {}
    lay['norm_mix_g'] = _row(p['norm_mix_g'][i])
    lay['w_in'] = jnp.pad(p['w_in'][i], ((0, 0), (0, IN_PROJ_PAD - p['w_in'].shape[-1]))).astype(BF16)
    lay['s5'] = _s5_prepare(
        p['s5_b_re'][i], p['s5_b_im'][i],
        [(p['s5_lam_re_f'][i], p['s5_lam_im_f'][i], p['s5_log_step_f'][i], p['s5_c_re_f'][i], p['s5_c_im_f'][i]),
         (p['s5_lam_re_b'][i], p['s5_lam_im_b'][i], p['s5_log_step_b'][i], p['s5_c_re_b'][i], p['s5_c_im_b'][i])])
    lay['s5_d'] = _row(p['s5_d'][i])
    lay['s5_w_glu'] = p['s5_w_glu'][i].astype(BF16)
    lay['s5_b_glu'] = _row(p['s5_b_glu'][i])
    lay['fnet_ab'] = _fnet_fold(p['fnet_w'][i])
    lay['fnet_b'] = _row(p['fnet_b'][i])
    lay['conv_w'] = p['ssd_conv_w'][i].astype(F32)
    lay['conv_b'] = _row(p['ssd_conv_b'][i])
    par = jnp.stack([p['ssd_dt_bias_f'][i], p['ssd_dt_bias_b'][i], p['ssd_a_log_f'][i], p['ssd_a_log_b'][i]]).astype(F32)
    lay['ssd_part'] = par.T
    lay['ssd_d'] = _row(jnp.repeat(p['ssd_d'][i], SSD_HEAD_DIM))
    lay['ssd_norm_g'] = _row(p['ssd_norm_g'][i])
    lay['w_out'] = p['w_out'][i].astype(BF16)
    lay['norm_ffn_g'] = _row(p['norm_ffn_g'][i])
    lay['w_gate'] = p['w_gate'][i].astype(BF16)
    lay['w_up'] = p['w_up'][i].astype(BF16)
    lay['w_down'] = p['w_down'][i].astype(BF16)
    return lay


def _trunk(x3, layers, final_g):
    b, l, dm = x3.shape
    x = x3.reshape(b * l, dm)
    for i, lay in enumerate(layers):
        u, v, z, xbc_c, dt = _norm_inproj(x, lay['norm_mix_g'], lay['w_in'], lay['conv_w'], lay['conv_b'], l)
        s5f, s5b = _s5_scan(u.reshape(b, l, -1), *lay['s5'])
        fn = _fnet(v, b, *lay['fnet_ab'], lay['fnet_b'])
        sdf, sdb = _ssd_scan(xbc_c.reshape(b, l, -1), dt.reshape(b, l, -1), lay['ssd_part'])
        n = b * l
        consts = (lay['s5_d'], lay['s5_w_glu'], lay['s5_b_glu'], lay['ssd_d'], lay['ssd_norm_g'], lay['w_out'],
                  lay['norm_ffn_g'], lay['w_gate'], lay['w_up'], lay['w_down'], final_g)
        x = _mix_ffn(x, s5f.reshape(n, -1), s5b.reshape(n, -1), u, fn,
                     sdf.reshape(n, -1), sdb.reshape(n, -1), xbc_c, z, consts,
                     final_norm=(i == len(layers) - 1))
    return x.reshape(b, l, dm)


def kernel(x_prompt, x_sample, norm_mix_g, w_in, s5_b_re, s5_b_im, s5_lam_re_f, s5_lam_im_f, s5_log_step_f, s5_c_re_f, s5_c_im_f, s5_lam_re_b, s5_lam_im_b, s5_log_step_b, s5_c_re_b, s5_c_im_b, s5_d, s5_w_glu, s5_b_glu, fnet_w, fnet_b, ssd_conv_w, ssd_conv_b, ssd_a_log_f, ssd_dt_bias_f, ssd_a_log_b, ssd_dt_bias_b, ssd_d, ssd_norm_g, w_out, norm_ffn_g, w_gate, w_up, w_down, final_norm_g):
    p = dict(norm_mix_g=norm_mix_g, w_in=w_in, s5_b_re=s5_b_re, s5_b_im=s5_b_im,
             s5_lam_re_f=s5_lam_re_f, s5_lam_im_f=s5_lam_im_f, s5_log_step_f=s5_log_step_f,
             s5_c_re_f=s5_c_re_f, s5_c_im_f=s5_c_im_f,
             s5_lam_re_b=s5_lam_re_b, s5_lam_im_b=s5_lam_im_b, s5_log_step_b=s5_log_step_b,
             s5_c_re_b=s5_c_re_b, s5_c_im_b=s5_c_im_b,
             s5_d=s5_d, s5_w_glu=s5_w_glu, s5_b_glu=s5_b_glu, fnet_w=fnet_w, fnet_b=fnet_b,
             ssd_conv_w=ssd_conv_w, ssd_conv_b=ssd_conv_b, ssd_a_log_f=ssd_a_log_f,
             ssd_dt_bias_f=ssd_dt_bias_f, ssd_a_log_b=ssd_a_log_b, ssd_dt_bias_b=ssd_dt_bias_b,
             ssd_d=ssd_d, ssd_norm_g=ssd_norm_g, w_out=w_out, norm_ffn_g=norm_ffn_g,
             w_gate=w_gate, w_up=w_up, w_down=w_down)
    layers = [_prepare_layer(i, p) for i in range(norm_mix_g.shape[0])]
    final_g = _row(final_norm_g)
    return (_trunk(x_prompt, layers, final_g), _trunk(x_sample, layers, final_g))
```

```python
import functools
import math

import numpy as np
import jax
import jax.numpy as jnp
from jax import lax
from jax.experimental import pallas as pl
from jax.experimental.pallas import tpu as pltpu

F32 = jnp.float32
BF16 = jnp.bfloat16

D_MODEL = 1024
S5_WIDTH = 256
S5_GROUP = 16
S5_GROUPS = 16
S5_STATE = 64
S5_NSTATE = S5_GROUPS * S5_STATE
FNET_WIDTH = 256
FNET_GROUPS = 4
FNET_GROUP = 64
SSD_WIDTH = 512
SSD_HEAD_DIM = 64
SSD_HEADS = 8
SSD_GROUPS = 2
SSD_STATE = 128
SSD_CONV = 5
SSD_BC = SSD_GROUPS * SSD_STATE
SSD_CONV_DIM = SSD_WIDTH + 2 * SSD_BC
DT_PAD = 128
IN_PROJ_PAD = S5_WIDTH + FNET_WIDTH + SSD_WIDTH + SSD_CONV_DIM + DT_PAD
D_FF = 2816
EPS = 1e-6
LOG2_E = math.log2(math.e)

LANES = 128
SUBLANES = 8
VMEM_LIMIT = 56 * 1024 * 1024

TOKEN_TILE = 512
CONV_BLOCK = 256
FFN_TILE = 512
S5_CHUNK = 256
S5_SEQS = 2
SSD_CHUNK = 128
SSD_SUB = 8
FNET_L2 = 128
FNET_STAGE1_ROWS = 512


def _rmsnorm(x, g):
    return x * lax.rsqrt(jnp.mean(x * x, axis=-1, keepdims=True) + EPS) * g


def _split3(v):
    hi = v.astype(BF16)
    r = v - hi.astype(F32)
    mid = r.astype(BF16)
    lo = (r - mid.astype(F32)).astype(BF16)
    return hi, mid, lo


def _dot(a, b):
    return jnp.dot(a, b, preferred_element_type=F32)


def _dot_exact_lhs(m, v):
    hi, mid, lo = _split3(v)
    return _dot(m, hi) + _dot(m, mid) + _dot(m, lo)


def _dot_exact_rhs(v, m):
    hi, mid, lo = _split3(v)
    return _dot(hi, m) + _dot(mid, m) + _dot(lo, m)


def _dot_hilo(m_hi, m_lo, v):
    v_hi = v.astype(BF16)
    v_lo = (v - v_hi.astype(F32)).astype(BF16)
    return _dot(m_hi, v_hi) + _dot(m_hi, v_lo) + _dot(m_lo, v_hi)


def _const_spec(shape):
    nd = len(shape)
    return pl.BlockSpec(shape, lambda *_: (0,) * nd, pipeline_mode=pl.Buffered(1))


def _params(semantics):
    return pltpu.CompilerParams(dimension_semantics=semantics, vmem_limit_bytes=VMEM_LIMIT)


_INPROJ_SPLITS = (0, S5_WIDTH, S5_WIDTH + FNET_WIDTH, S5_WIDTH + FNET_WIDTH + SSD_WIDTH,
                  S5_WIDTH + FNET_WIDTH + SSD_WIDTH + SSD_CONV_DIM, IN_PROJ_PAD)


def _norm_inproj_kernel(prev_ref, x_ref, next_ref, g_ref, w_ref, cw_ref, cb_ref,
                        u_ref, v_ref, z_ref, xbc_ref, dt_ref, *, tiles_per_seq):
    i = pl.program_id(0)
    t_len = x_ref.shape[0]
    pad = SSD_CONV // 2
    rows = t_len + 2 * SUBLANES
    y_ext = _rmsnorm(jnp.concatenate([prev_ref[...], x_ref[...], next_ref[...]], axis=0), g_ref[...])
    h = y_ext[SUBLANES:SUBLANES + t_len].astype(BF16)
    h_ext = y_ext.astype(BF16)
    lo_xbc = _INPROJ_SPLITS[3]
    first = i % tiles_per_seq == 0
    last = i % tiles_per_seq == tiles_per_seq - 1

    def other_dot(which):
        lo, hi = _INPROJ_SPLITS[which], _INPROJ_SPLITS[which + 1]
        res = _dot(h, w_ref[:, lo:hi])
        if which == 1:
            for hf in range(FNET_WIDTH // LANES):
                v_ref[hf] = res[:, hf * LANES:(hf + 1) * LANES]
        else:
            (u_ref, None, z_ref, None, dt_ref)[which][...] = res

    others = [0, 1, 2, 4]
    for j in range(SSD_CONV_DIM // CONV_BLOCK):
        cols = slice(j * CONV_BLOCK, (j + 1) * CONV_BLOCK)
        d = _dot(h_ext, w_ref[:, lo_xbc + j * CONV_BLOCK:lo_xbc + (j + 1) * CONV_BLOCK])
        xe = jnp.concatenate([jnp.where(first, 0.0, d[0:SUBLANES]), d[SUBLANES:SUBLANES + t_len],
                              jnp.where(last, 0.0, d[SUBLANES + t_len:])], axis=0)
        if others:
            other_dot(others.pop(0))
        acc = jnp.broadcast_to(cb_ref[:, cols], (t_len, CONV_BLOCK))
        for k in range(SSD_CONV):
            tap = xe if k == pad else pltpu.roll(xe, (pad - k) % rows, axis=0)
            acc = acc + cw_ref[k:k + 1, cols] * tap[SUBLANES:SUBLANES + t_len]
        xbc_ref[:, cols] = acc * jax.nn.sigmoid(acc)
    while others:
        other_dot(others.pop(0))


def _norm_inproj(x, g, w, conv_w, conv_b, seq_len):
    n = x.shape[0]
    widths = [hi - lo for lo, hi in zip(_INPROJ_SPLITS[:-1], _INPROJ_SPLITS[1:])]
    per = TOKEN_TILE // SUBLANES
    last = n // SUBLANES - 1
    halves = FNET_WIDTH // LANES
    out_shape = [jax.ShapeDtypeStruct((n, wd), F32) for wd in widths]
    out_specs = [pl.BlockSpec((TOKEN_TILE, wd), lambda i: (i, 0)) for wd in widths]
    out_shape[1] = jax.ShapeDtypeStruct((halves, n, LANES), F32)
    out_specs[1] = pl.BlockSpec((halves, TOKEN_TILE, LANES), lambda i: (0, i, 0))
    return pl.pallas_call(
        functools.partial(_norm_inproj_kernel, tiles_per_seq=seq_len // TOKEN_TILE),
        out_shape=out_shape,
        grid=(n // TOKEN_TILE,),
        in_specs=[pl.BlockSpec((SUBLANES, D_MODEL), lambda i: (jnp.maximum(i * per - 1, 0), 0)),
                  pl.BlockSpec((TOKEN_TILE, D_MODEL), lambda i: (i, 0)),
                  pl.BlockSpec((SUBLANES, D_MODEL), lambda i: (jnp.minimum((i + 1) * per, last), 0)),
                  _const_spec((1, D_MODEL)), _const_spec((D_MODEL, IN_PROJ_PAD)),
                  _const_spec(conv_w.shape), _const_spec(conv_b.shape)],
        out_specs=out_specs,
        compiler_params=_params(("parallel",)),
        name="norm_inproj",
    )(x, x, x, g, w, conv_w, conv_b)


def _s5_scan_kernel(uf_ref, ub_ref, bmat_ref, cmat_ref, lam_ref, yf_ref, yb_ref, bu0, bu1, h0, h1, carry):
    s = pl.program_id(1)
    nseq, t_len = uf_ref.shape[0], uf_ref.shape[1] // 2
    chains = [(i, d) for i in range(nseq) for d in range(2)]
    lam = [lam_ref[k] for k in range(4)]

    @pl.when(s == 0)
    def _():
        carry[...] = jnp.zeros_like(carry)
        bu1[...] = jnp.zeros_like(bu1)
        h0[...] = jnp.zeros_like(h0)

    def rows_of(half, d):
        lo = half * t_len if d == 0 else (1 - half) * t_len
        return pl.ds(lo, t_len)

    def project(half, bu, i, d):
        u16 = (uf_ref if d == 0 else ub_ref)[i, rows_of(half, d), :].astype(BF16)
        for part in range(2):
            col = (2 * d + part) * S5_NSTATE
            prod = _dot(u16, bmat_ref[:, col:col + S5_NSTATE])
            k = (i * 2 + d) * 2 + part
            for j in range(SUBLANES):
                bu[k, pl.ds(j, t_len, stride=SUBLANES), :] = prod[:, j * LANES:(j + 1) * LANES]

    def scan(bu, hb, hs, t_lo, t_hi):
        for t in range(t_lo, t_hi):
            for n, (i, d) in enumerate(chains):
                kre = (i * 2 + d) * 2
                row = pl.ds((t if d == 0 else t_len - 1 - t) * SUBLANES, SUBLANES)
                a_re, a_im = lam[2 * d], lam[2 * d + 1]
                h_re, h_im = hs[2 * n], hs[2 * n + 1]
                hs[2 * n] = a_re * h_re - a_im * h_im + bu[kre, row, :]
                hs[2 * n + 1] = a_re * h_im + a_im * h_re + bu[kre + 1, row, :]
                hb[kre, row, :] = hs[2 * n]
                hb[kre + 1, row, :] = hs[2 * n + 1]

    def read_out(half, hb, i, d):
        kre = (i * 2 + d) * 2
        h = jnp.concatenate(
            [hb[kre + part, pl.ds(j, t_len, stride=SUBLANES), :] for part in range(2) for j in range(SUBLANES)],
            axis=1).astype(BF16)
        (yf_ref if d == 0 else yb_ref)[i, rows_of(half, d), :] = _dot(h, cmat_ref[d])

    hs = [carry[k] for k in range(4 * nseq)]
    per = t_len // len(chains)
    for half, (bu_w, bu_r, h_w, h_r) in enumerate(((bu0, bu1, h1, h0), (bu1, bu0, h0, h1))):
        for n, (i, d) in enumerate(chains):
            project(half, bu_w, i, d)
            scan(bu_r, h_w, hs, n * per, (n + 1) * per)
            read_out(half, h_r, i, d)
    for k in range(4 * nseq):
        carry[k] = hs[k]


def _s5_scan(u, bmat, cmat, lam):
    b, l, _ = u.shape
    nb = l // (2 * S5_CHUNK)
    blk = (S5_SEQS, 2 * S5_CHUNK, S5_WIDTH)
    in_f = pl.BlockSpec(blk, lambda i, s: (i, jnp.minimum(s, nb - 1), 0))
    in_b = pl.BlockSpec(blk, lambda i, s: (i, nb - 1 - jnp.minimum(s, nb - 1), 0))
    out_f = pl.BlockSpec(blk, lambda i, s: (i, jnp.maximum(s - 1, 0), 0))
    out_b = pl.BlockSpec(blk, lambda i, s: (i, nb - 1 - jnp.maximum(s - 1, 0), 0))
    buf = pltpu.VMEM((S5_SEQS * 4, S5_CHUNK * SUBLANES, LANES), F32)
    return pl.pallas_call(
        _s5_scan_kernel,
        out_shape=[jax.ShapeDtypeStruct(u.shape, F32)] * 2,
        grid=(b // S5_SEQS, nb + 1),
        in_specs=[in_f, in_b, _const_spec(bmat.shape), _const_spec(cmat.shape), _const_spec(lam.shape)],
        out_specs=[out_f, out_b],
        scratch_shapes=[buf, buf, buf, buf, pltpu.VMEM((S5_SEQS * 4, SUBLANES, LANES), F32)],
        compiler_params=_params(("parallel", "arbitrary")),
        name="s5_scan",
    )(u, u, bmat, cmat, lam)


def _s5_prepare(b_re, b_im, dirs):
    g, p, c = S5_GROUPS, S5_STATE, S5_GROUP
    eye = jnp.eye(g, dtype=F32)
    bcols, cmats, lams = [], [], []
    for lam_re, lam_im, log_step, c_re, c_im in dirs:
        step = jnp.exp(log_step)[:, None]
        mag = jnp.exp(lam_re * step)
        lb_re, lb_im = mag * jnp.cos(lam_im * step), mag * jnp.sin(lam_im * step)
        den = lam_re * lam_re + lam_im * lam_im
        q_re = ((lb_re - 1.0) * lam_re + lb_im * lam_im) / den
        q_im = (lb_im * lam_re - (lb_re - 1.0) * lam_im) / den
        bb_re = q_re[..., None] * b_re - q_im[..., None] * b_im
        bb_im = q_re[..., None] * b_im + q_im[..., None] * b_re
        for bb in (bb_re, bb_im):
            bcols.append(jnp.einsum('gpc,gh->gchp', bb, eye).reshape(g * c, g * p))
        cm = [jnp.einsum('gcp,gh->gphc', cc, eye).reshape(g * p, g * c) for cc in (c_re, -c_im)]
        cmats.append(jnp.concatenate(cm, axis=0))
        lams += [lb_re.reshape(SUBLANES, LANES), lb_im.reshape(SUBLANES, LANES)]
    return (jnp.concatenate(bcols, axis=1).astype(BF16), jnp.stack(cmats).astype(BF16), jnp.stack(lams))


def _dft_parts(n, scale):
    k = np.arange(n)
    ang = 2.0 * np.pi * ((k[:, None] * k[None, :]) % n) / n
    return np.cos(ang) * scale, np.sin(ang) * scale


def _hilo(m):
    m = np.asarray(m, np.float32)
    hi = jnp.asarray(m, F32).astype(BF16)
    lo = (jnp.asarray(m, F32) - hi.astype(F32)).astype(BF16)
    return hi, lo


def _fnet_fold_kernel(ch_ref, cl_ref, sh_ref, sl_ref, w_ref, a_ref, b_ref):
    w = w_ref[...]
    a_ref[...] = _dot_hilo(ch_ref[...], cl_ref[...], w).astype(BF16)
    b_ref[...] = _dot_hilo(sh_ref[...], sl_ref[...], w).astype(BF16)


def _fnet_fold(w):
    c64, s64 = _dft_parts(FNET_GROUP, 1.0 / math.sqrt(FNET_GROUP))
    eye = np.eye(FNET_GROUPS)
    cbd, sbd = np.kron(eye, c64), np.kron(eye, s64)
    wbd = jnp.einsum('gcd,gh->gchd', w, jnp.eye(FNET_GROUPS, dtype=F32)).reshape(FNET_WIDTH, FNET_WIDTH)
    shape = (FNET_WIDTH, FNET_WIDTH)
    return pl.pallas_call(
        _fnet_fold_kernel,
        out_shape=[jax.ShapeDtypeStruct(shape, BF16)] * 2,
        name="fnet_fold",
    )(*_hilo(cbd), *_hilo(sbd), wbd)


def _fnet_stage1_kernel(x_ref, fh_ref, fl_ref, gre_ref, gim_ref, xs, gs):
    halves, l1, nb = x_ref.shape[0], x_ref.shape[1], x_ref.shape[2]
    for hf in range(halves):
        xs[hf] = x_ref[hf].reshape(l1 * nb, LANES)
    order = [(hf, s) for hf in range(halves) for s in range(nb)]
    g = _dot_hilo(fh_ref[...], fl_ref[...],
                  jnp.concatenate([xs[hf, pl.ds(s, l1, stride=nb), :] for hf, s in order], axis=1))
    for idx, (hf, s) in enumerate(order):
        gs[2 * hf, pl.ds(s, l1, stride=nb), :] = g[:l1, idx * LANES:(idx + 1) * LANES]
        gs[2 * hf + 1, pl.ds(s, l1, stride=nb), :] = g[l1:, idx * LANES:(idx + 1) * LANES]
    for hf in range(halves):
        gre_ref[hf] = gs[2 * hf].reshape(l1, nb, LANES)
        gim_ref[hf] = gs[2 * hf + 1].reshape(l1, nb, LANES)


def _fnet_stage2_kernel(gre_ref, gim_ref, twr_ref, twi_ref, f_ref, a_ref, b_ref, bias_ref, o_ref, os):
    halves, kb, l2 = gre_ref.shape[0], gre_ref.shape[1], gre_ref.shape[2]
    cols = []
    for i in range(kb):
        tr, ti = twr_ref[i], twi_ref[i]
        for hf in range(halves):
            gr, gi = gre_ref[hf, i], gim_ref[hf, i]
            cols.append(jnp.concatenate([gr * tr - gi * ti, gr * ti + gi * tr], axis=0))
    y = _dot(f_ref[...], jnp.concatenate(cols, axis=1).astype(BF16))
    y_re = jnp.concatenate([y[:l2, i * FNET_WIDTH:(i + 1) * FNET_WIDTH] for i in range(kb)], axis=0)
    y_im = jnp.concatenate([y[l2:, i * FNET_WIDTH:(i + 1) * FNET_WIDTH] for i in range(kb)], axis=0)
    out = _dot(y_re.astype(BF16), a_ref[...]) + _dot(y_im.astype(BF16), b_ref[...]) + bias_ref[...]
    for i in range(kb):
        for hf in range(halves):
            os[hf, pl.ds(i, l2, stride=kb), :] = out[i * l2:(i + 1) * l2, hf * LANES:(hf + 1) * LANES]
    for hf in range(halves):
        o_ref[hf] = os[hf].reshape(l2, kb, LANES)


def _fnet(v, b, a, bm, bias):
    halves, n, _ = v.shape
    l = n // b
    l2 = FNET_L2
    l1 = l // l2
    c1, s1 = _dft_parts(l1, 1.0 / math.sqrt(l1))
    f1h, f1l = _hilo(np.concatenate([c1, -s1], axis=0))
    c2, s2 = _dft_parts(l2, 1.0 / math.sqrt(l2))
    f2 = jnp.asarray(np.block([[c2, s2], [-s2, c2]]), F32).astype(BF16)
    m = (np.arange(l1)[:, None] * np.arange(l2)[None, :]) % l
    ang = 2.0 * np.pi * m / l
    twr = jnp.broadcast_to(jnp.asarray(np.cos(ang), F32)[:, :, None], (l1, l2, LANES))
    twi = jnp.broadcast_to(jnp.asarray(-np.sin(ang), F32)[:, :, None], (l1, l2, LANES))

    nb = FNET_STAGE1_ROWS // l1
    xs = pl.BlockSpec((halves, None, l1, nb, LANES), lambda i, j: (0, i, 0, j, 0))
    g_shape = jax.ShapeDtypeStruct((halves, b, l1, l2, LANES), F32)
    gre, gim = pl.pallas_call(
        _fnet_stage1_kernel,
        out_shape=[g_shape, g_shape],
        grid=(b, l2 // nb),
        in_specs=[xs, _const_spec(f1h.shape), _const_spec(f1l.shape)],
        out_specs=[xs, xs],
        scratch_shapes=[pltpu.VMEM((halves, l1 * nb, LANES), F32), pltpu.VMEM((2 * halves, l1 * nb, LANES), F32)],
        compiler_params=_params(("parallel", "parallel")),
        name="fnet_stage1",
    )(v.reshape(halves, b, l1, l2, LANES), f1h, f1l)

    kb = SUBLANES
    gs = pl.BlockSpec((halves, None, kb, l2, LANES), lambda i, j: (0, i, j, 0, 0))
    ts = pl.BlockSpec((kb, l2, LANES), lambda i, j: (j, 0, 0))
    y = pl.pallas_call(
        _fnet_stage2_kernel,
        out_shape=jax.ShapeDtypeStruct((halves, b, l2, l1, LANES), F32),
        grid=(b, l1 // kb),
        in_specs=[gs, gs, ts, ts, _const_spec(f2.shape),
                  _const_spec(a.shape), _const_spec(bm.shape), _const_spec(bias.shape)],
        out_specs=pl.BlockSpec((halves, None, l2, kb, LANES), lambda i, j: (0, i, 0, j, 0)),
        scratch_shapes=[pltpu.VMEM((halves, l2 * kb, LANES), F32)],
        compiler_params=_params(("parallel", "parallel")),
        name="fnet_stage2",
    )(gre, gim, twr, twi, f2, a, bm, bias)
    return y.reshape(halves, n, LANES)


def _ones_where(mask):
    return jnp.where(mask, 1.0, 0.0).astype(BF16)


def _softplus(x):
    return jnp.maximum(x, 0.0) + jnp.log1p(jnp.exp(-jnp.abs(x)))


def _ssd_dt_selector():
    sel = np.zeros((2, LANES, SSD_WIDTH), np.float32)
    for d in range(2):
        for lane in range(SSD_WIDTH):
            sel[d, 2 * d * SSD_HEADS + lane // SSD_HEAD_DIM, lane] = 1.0
    return jnp.asarray(sel, BF16)


def _ssd_scan_kernel(xf_ref, xb_ref, dtf_ref, dtb_ref, part_ref, sel_ref, yf_ref, yb_ref, state):
    c = pl.program_id(1)
    t_len = SSD_CHUNK
    n_sub = xf_ref.shape[1] // t_len
    pair_w = 2 * SSD_HEAD_DIM
    heads_per_pair = pair_w // SSD_HEAD_DIM
    n_pairs = SSD_WIDTH // pair_w
    pairs_per_group = n_pairs // SSD_GROUPS

    @pl.when(c == 0)
    def _():
        state[...] = jnp.zeros_like(state)

    row = lax.broadcasted_iota(jnp.int32, (t_len, t_len), 0)
    col = lax.broadcasted_iota(jnp.int32, (t_len, t_len), 1)
    lower, upper = _ones_where(row >= col), _ones_where(row <= col)
    first_head = lax.broadcasted_iota(jnp.int32, (1, pair_w), 1) < SSD_HEAD_DIM
    states = [[state[d, q] for q in range(n_pairs)] for d in range(2)]

    for k in range(n_sub):
        at = [k * t_len, (n_sub - 1 - k) * t_len]
        rows = []
        for d, dt_ref in enumerate((dtf_ref, dtb_ref)):
            dt_row = _softplus(dt_ref[0, at[d]:at[d] + t_len, :].T[0:SSD_HEADS] + part_ref[:, d:d + 1])
            rows += [dt_row, dt_row * (-LOG2_E * jnp.exp(part_ref[:, 2 + d:3 + d]))]
        n_rows = len(rows) * SSD_HEADS
        row_form = jnp.concatenate(rows, axis=0)
        col_form = jnp.concatenate([row_form, jnp.zeros((LANES - n_rows, t_len), F32)], axis=0).T
        cum_col = _dot_exact_lhs(lower, col_form)
        cum_row = _dot_exact_rhs(row_form, upper)
        cf_hi = col_form.astype(BF16)
        cf_lo = (col_form - cf_hi.astype(F32)).astype(BF16)

        for d, (x_ref, y_ref) in enumerate(((xf_ref, yf_ref), (xb_ref, yb_ref))):
            da_at = (2 * d + 1) * SSD_HEADS
            if d == 0:
                keep = row >= col
                p_col = cum_col
                p_row = cum_row[da_at:da_at + SSD_HEADS]
            else:
                keep = col >= row
                p_col = col_form - cum_col
                p_row = row_form[da_at:da_at + SSD_HEADS] - cum_row[da_at:da_at + SSD_HEADS]

            xbc = x_ref[0, at[d]:at[d] + t_len, :]
            dt_wide = _dot(cf_hi, sel_ref[d]) + _dot(cf_lo, sel_ref[d])
            for g in range(SSD_GROUPS):
                b_g = xbc[:, SSD_WIDTH + g * SSD_STATE:SSD_WIDTH + (g + 1) * SSD_STATE]
                c_lo = SSD_WIDTH + SSD_BC + g * SSD_STATE
                c_g = xbc[:, c_lo:c_lo + SSD_STATE].astype(BF16)
                bt_g = b_g.T.astype(BF16)
                cb = _dot(c_g, bt_g)
                group_pairs = range(g * pairs_per_group, (g + 1) * pairs_per_group)
                off_all = _dot(c_g, jnp.concatenate([states[d][q] for q in group_pairs], axis=1).astype(BF16))
                totals, weighted = [], []
                for j, q in enumerate(group_pairs):
                    h0 = q * heads_per_pair

                    def expand(m):
                        return jnp.where(first_head, m[:, da_at + h0:da_at + h0 + 1],
                                         m[:, da_at + h0 + 1:da_at + h0 + 2])

                    p_e = expand(p_col)
                    xdt = xbc[:, q * pair_w:(q + 1) * pair_w] * dt_wide[:, q * pair_w:(q + 1) * pair_w]
                    if d == 0:
                        total = p_e[t_len - 1:t_len]
                        off_scale = jnp.exp2(p_e)
                        w_state = jnp.exp2(total - p_e)
                    else:
                        total = expand(col_form[t_len - 1:t_len]) - p_e[t_len - 1:t_len]
                        off_scale = jnp.exp2(total + p_e)
                        w_state = jnp.exp2(-p_e)
                    y = off_all[:, j * pair_w:(j + 1) * pair_w] * off_scale
                    xdt16 = xdt.astype(BF16)
                    intra = []
                    for h in range(h0, h0 + heads_per_pair):
                        diff = p_col[:, da_at + h:da_at + h + 1] - p_row[h:h + 1, :]
                        m = cb * jnp.exp2(jnp.where(keep, diff, -jnp.inf))
                        intra.append(_dot(m.astype(BF16), xdt16))
                    y_ref[0, at[d]:at[d] + t_len, q * pair_w:(q + 1) * pair_w] = (
                        y + jnp.where(first_head, intra[0], intra[1]))
                    totals.append(total)
                    weighted.append((xdt * w_state).astype(BF16))
                update = _dot(bt_g, jnp.concatenate(weighted, axis=1))
                for j, q in enumerate(group_pairs):
                    states[d][q] = (states[d][q] * jnp.exp2(totals[j])
                                    + update[:, j * pair_w:(j + 1) * pair_w])

    for d in range(2):
        for q in range(n_pairs):
            state[d, q] = states[d][q]


def _ssd_scan(xbc, dt, part):
    b, l, ch = xbc.shape
    t_blk = SSD_CHUNK * SSD_SUB
    nb = l // t_blk
    pairs = SSD_WIDTH // (2 * SSD_HEAD_DIM)
    sel = _ssd_dt_selector()

    def fwd(width):
        return pl.BlockSpec((1, t_blk, width), lambda i, c: (i, c, 0))

    def bwd(width):
        return pl.BlockSpec((1, t_blk, width), lambda i, c: (i, nb - 1 - c, 0))

    return pl.pallas_call(
        _ssd_scan_kernel,
        out_shape=[jax.ShapeDtypeStruct((b, l, SSD_WIDTH), F32)] * 2,
        grid=(b, nb),
        in_specs=[fwd(ch), bwd(ch), fwd(DT_PAD), bwd(DT_PAD), _const_spec(part.shape), _const_spec(sel.shape)],
        out_specs=[fwd(SSD_WIDTH), bwd(SSD_WIDTH)],
        scratch_shapes=[pltpu.VMEM((2, pairs, SSD_STATE, 2 * SSD_HEAD_DIM), F32)],
        compiler_params=_params(("parallel", "arbitrary")),
        name="ssd_scan",
    )(xbc, xbc, dt, dt, part, sel)


def _mix_ffn_kernel(x_ref, s5f_ref, s5b_ref, u_ref, fn_ref, sdf_ref, sdb_ref, xc_ref, z_ref,
                    s5d_ref, wglu_ref, bglu_ref, sdd_ref, sdg_ref, wo_ref,
                    g_ref, wg_ref, wu_ref, wd_ref, gf_ref, o_ref, *, final_norm):
    ya = s5f_ref[...] + s5b_ref[...] + u_ref[...] * s5d_ref[...]
    ga = jax.nn.gelu(ya)
    ya = ga * jax.nn.sigmoid(_dot(ga.astype(BF16), wglu_ref[...]) + bglu_ref[...])
    z = z_ref[...]
    yc = (sdf_ref[...] + sdb_ref[...] + xc_ref[...] * sdd_ref[...]) * (z * jax.nn.sigmoid(z))
    yc = _rmsnorm(yc, sdg_ref[...])
    lo, mid = S5_WIDTH, S5_WIDTH + FNET_WIDTH
    x = (x_ref[...] + _dot(ya.astype(BF16), wo_ref[0:lo, :])
         + _dot(jnp.concatenate([fn_ref[hf] for hf in range(fn_ref.shape[0])], axis=1).astype(BF16), wo_ref[lo:mid, :])
         + _dot(yc.astype(BF16), wo_ref[mid:, :]))
    h = _rmsnorm(x, g_ref[...]).astype(BF16)
    gate = _dot(h, wg_ref[...])
    act = (gate * jax.nn.sigmoid(gate) * _dot(h, wu_ref[...])).astype(BF16)
    y = x + _dot(act, wd_ref[...])
    o_ref[...] = _rmsnorm(y, gf_ref[...]) if final_norm else y


def _mix_ffn(x, s5f, s5b, u, fn, sdf, sdb, xbc_c, z, consts, final_norm):
    n = x.shape[0]

    def rows(width):
        return pl.BlockSpec((FFN_TILE, width), lambda i: (i, 0))

    return pl.pallas_call(
        functools.partial(_mix_ffn_kernel, final_norm=final_norm),
        out_shape=jax.ShapeDtypeStruct(x.shape, F32),
        grid=(n // FFN_TILE,),
        in_specs=[rows(D_MODEL), rows(S5_WIDTH), rows(S5_WIDTH), rows(S5_WIDTH),
                  pl.BlockSpec((fn.shape[0], FFN_TILE, LANES), lambda i: (0, i, 0)),
                  rows(SSD_WIDTH), rows(SSD_WIDTH), rows(SSD_WIDTH), rows(SSD_WIDTH)]
                 + [_const_spec(a.shape) for a in consts],
        out_specs=rows(D_MODEL),
        compiler_params=_params(("parallel",)),
        name="mix_ffn",
    )(x, s5f, s5b, u, fn, sdf, sdb, xbc_c, z, *consts)


def _row(v):
    return v.reshape(1, -1).astype(F32)


def _prepare_layer(i, p):
    lay = {}
    lay['norm_mix_g'] = _row(p['norm_mix_g'][i])
    lay['w_in'] = jnp.pad(p['w_in'][i], ((0, 0), (0, IN_PROJ_PAD - p['w_in'].shape[-1]))).astype(BF16)
    lay['s5'] = _s5_prepare(
        p['s5_b_re'][i], p['s5_b_im'][i],
        [(p['s5_lam_re_f'][i], p['s5_lam_im_f'][i], p['s5_log_step_f'][i], p['s5_c_re_f'][i], p['s5_c_im_f'][i]),
         (p['s5_lam_re_b'][i], p['s5_lam_im_b'][i], p['s5_log_step_b'][i], p['s5_c_re_b'][i], p['s5_c_im_b'][i])])
    lay['s5_d'] = _row(p['s5_d'][i])
    lay['s5_w_glu'] = p['s5_w_glu'][i].astype(BF16)
    lay['s5_b_glu'] = _row(p['s5_b_glu'][i])
    lay['fnet_ab'] = _fnet_fold(p['fnet_w'][i])
    lay['fnet_b'] = _row(p['fnet_b'][i])
    lay['conv_w'] = p['ssd_conv_w'][i].astype(F32)
    lay['conv_b'] = _row(p['ssd_conv_b'][i])
    par = jnp.stack([p['ssd_dt_bias_f'][i], p['ssd_dt_bias_b'][i], p['ssd_a_log_f'][i], p['ssd_a_log_b'][i]]).astype(F32)
    lay['ssd_part'] = par.T
    lay['ssd_d'] = _row(jnp.repeat(p['ssd_d'][i], SSD_HEAD_DIM))
    lay['ssd_norm_g'] = _row(p['ssd_norm_g'][i])
    lay['w_out'] = p['w_out'][i].astype(BF16)
    lay['norm_ffn_g'] = _row(p['norm_ffn_g'][i])
    lay['w_gate'] = p['w_gate'][i].astype(BF16)
    lay['w_up'] = p['w_up'][i].astype(BF16)
    lay['w_down'] = p['w_down'][i].astype(BF16)
    return lay


def _trunk(x3, layers, final_g):
    b, l, dm = x3.shape
    x = x3.reshape(b * l, dm)
    for i, lay in enumerate(layers):
        u, v, z, xbc_c, dt = _norm_inproj(x, lay['norm_mix_g'], lay['w_in'], lay['conv_w'], lay['conv_b'], l)
        s5f, s5b = _s5_scan(u.reshape(b, l, -1), *lay['s5'])
        fn = _fnet(v, b, *lay['fnet_ab'], lay['fnet_b'])
        sdf, sdb = _ssd_scan(xbc_c.reshape(b, l, -1), dt.reshape(b, l, -1), lay['ssd_part'])
        n = b * l
        consts = (lay['s5_d'], lay['s5_w_glu'], lay['s5_b_glu'], lay['ssd_d'], lay['ssd_norm_g'], lay['w_out'],
                  lay['norm_ffn_g'], lay['w_gate'], lay['w_up'], lay['w_down'], final_g)
        x = _mix_ffn(x, s5f.reshape(n, -1), s5b.reshape(n, -1), u, fn,
                     sdf.reshape(n, -1), sdb.reshape(n, -1), xbc_c, z, consts,
                     final_norm=(i == len(layers) - 1))
    return x.reshape(b, l, dm)


def kernel(x_prompt, x_sample, norm_mix_g, w_in, s5_b_re, s5_b_im, s5_lam_re_f, s5_lam_im_f, s5_log_step_f, s5_c_re_f, s5_c_im_f, s5_lam_re_b, s5_lam_im_b, s5_log_step_b, s5_c_re_b, s5_c_im_b, s5_d, s5_w_glu, s5_b_glu, fnet_w, fnet_b, ssd_conv_w, ssd_conv_b, ssd_a_log_f, ssd_dt_bias_f, ssd_a_log_b, ssd_dt_bias_b, ssd_d, ssd_norm_g, w_out, norm_ffn_g, w_gate, w_up, w_down, final_norm_g):
    p = dict(norm_mix_g=norm_mix_g, w_in=w_in, s5_b_re=s5_b_re, s5_b_im=s5_b_im,
             s5_lam_re_f=s5_lam_re_f, s5_lam_im_f=s5_lam_im_f, s5_log_step_f=s5_log_step_f,
             s5_c_re_f=s5_c_re_f, s5_c_im_f=s5_c_im_f,
             s5_lam_re_b=s5_lam_re_b, s5_lam_im_b=s5_lam_im_b, s5_log_step_b=s5_log_step_b,
             s5_c_re_b=s5_c_re_b, s5_c_im_b=s5_c_im_b,
             s5_d=s5_d, s5_w_glu=s5_w_glu, s5_b_glu=s5_b_glu, fnet_w=fnet_w, fnet_b=fnet_b,
             ssd_conv_w=ssd_conv_w, ssd_conv_b=ssd_conv_b, ssd_a_log_f=ssd_a_log_f,
             ssd_dt_bias_f=ssd_dt_bias_f, ssd_a_log_b=ssd_a_log_b, ssd_dt_bias_b=ssd_dt_bias_b,
             ssd_d=ssd_d, ssd_norm_g=ssd_norm_g, w_out=w_out, norm_ffn_g=norm_ffn_g,
             w_gate=w_gate, w_up=w_up, w_down=w_down)
    layers = [_prepare_layer(i, p) for i in range(norm_mix_g.shape[0])]
    final_g = _row(final_norm_g)
    return (_trunk(x_prompt, layers, final_g), _trunk(x_sample, layers, final_g))
```

```python
import functools
import math

import numpy as np
import jax
import jax.numpy as jnp
from jax import lax
from jax.experimental import pallas as pl
from jax.experimental.pallas import tpu as pltpu

F32 = jnp.float32
BF16 = jnp.bfloat16

D_MODEL = 1024
S5_WIDTH = 256
S5_GROUP = 16
S5_GROUPS = 16
S5_STATE = 64
S5_NSTATE = S5_GROUPS * S5_STATE
FNET_WIDTH = 256
FNET_GROUPS = 4
FNET_GROUP = 64
SSD_WIDTH = 512
SSD_HEAD_DIM = 64
SSD_HEADS = 8
SSD_GROUPS = 2
SSD_STATE = 128
SSD_CONV = 5
SSD_BC = SSD_GROUPS * SSD_STATE
SSD_CONV_DIM = SSD_WIDTH + 2 * SSD_BC
DT_PAD = 128
IN_PROJ_PAD = S5_WIDTH + FNET_WIDTH + SSD_WIDTH + SSD_CONV_DIM + DT_PAD
D_FF = 2816
EPS = 1e-6
LOG2_E = math.log2(math.e)

LANES = 128
SUBLANES = 8
VMEM_LIMIT = 56 * 1024 * 1024

TOKEN_TILE = 1024
CONV_BLOCK = 256
FFN_TILE = 512
S5_CHUNK = 256
S5_SEQS = 2
SSD_CHUNK = 128
SSD_SUB = 8
FNET_L2 = 128
FNET_STAGE1_ROWS = 512


def _rmsnorm(x, g):
    return x * lax.rsqrt(jnp.mean(x * x, axis=-1, keepdims=True) + EPS) * g


def _split3(v):
    hi = v.astype(BF16)
    r = v - hi.astype(F32)
    mid = r.astype(BF16)
    lo = (r - mid.astype(F32)).astype(BF16)
    return hi, mid, lo


def _dot(a, b):
    return jnp.dot(a, b, preferred_element_type=F32)


def _dot_exact_lhs(m, v):
    hi, mid, lo = _split3(v)
    return _dot(m, hi) + _dot(m, mid) + _dot(m, lo)


def _dot_exact_rhs(v, m):
    hi, mid, lo = _split3(v)
    return _dot(hi, m) + _dot(mid, m) + _dot(lo, m)


def _dot_hilo(m_hi, m_lo, v):
    v_hi = v.astype(BF16)
    v_lo = (v - v_hi.astype(F32)).astype(BF16)
    return _dot(m_hi, v_hi) + _dot(m_hi, v_lo) + _dot(m_lo, v_hi)


def _const_spec(shape):
    nd = len(shape)
    return pl.BlockSpec(shape, lambda *_: (0,) * nd, pipeline_mode=pl.Buffered(1))


def _params(semantics):
    return pltpu.CompilerParams(dimension_semantics=semantics, vmem_limit_bytes=VMEM_LIMIT)


_INPROJ_SPLITS = (0, S5_WIDTH, S5_WIDTH + FNET_WIDTH, S5_WIDTH + FNET_WIDTH + SSD_WIDTH,
                  S5_WIDTH + FNET_WIDTH + SSD_WIDTH + SSD_CONV_DIM, IN_PROJ_PAD)


def _norm_inproj_kernel(prev_ref, x_ref, next_ref, g_ref, w_ref, cw_ref, cb_ref,
                        u_ref, v_ref, z_ref, xbc_ref, dt_ref, *, tiles_per_seq):
    i = pl.program_id(0)
    t_len = x_ref.shape[0]
    pad = SSD_CONV // 2
    rows = t_len + 2 * SUBLANES
    y_ext = _rmsnorm(jnp.concatenate([prev_ref[...], x_ref[...], next_ref[...]], axis=0), g_ref[...])
    h = y_ext[SUBLANES:SUBLANES + t_len].astype(BF16)
    h_ext = y_ext.astype(BF16)
    lo_xbc = _INPROJ_SPLITS[3]
    first = i % tiles_per_seq == 0
    last = i % tiles_per_seq == tiles_per_seq - 1

    def other_dot(which):
        lo, hi = _INPROJ_SPLITS[which], _INPROJ_SPLITS[which + 1]
        res = _dot(h, w_ref[:, lo:hi])
        if which == 1:
            for hf in range(FNET_WIDTH // LANES):
                v_ref[hf] = res[:, hf * LANES:(hf + 1) * LANES]
        else:
            (u_ref, None, z_ref, None, dt_ref)[which][...] = res

    others = [0, 1, 2, 4]
    for j in range(SSD_CONV_DIM // CONV_BLOCK):
        cols = slice(j * CONV_BLOCK, (j + 1) * CONV_BLOCK)
        d = _dot(h_ext, w_ref[:, lo_xbc + j * CONV_BLOCK:lo_xbc + (j + 1) * CONV_BLOCK])
        xe = jnp.concatenate([jnp.where(first, 0.0, d[0:SUBLANES]), d[SUBLANES:SUBLANES + t_len],
                              jnp.where(last, 0.0, d[SUBLANES + t_len:])], axis=0)
        if others:
            other_dot(others.pop(0))
        acc = jnp.broadcast_to(cb_ref[:, cols], (t_len, CONV_BLOCK))
        for k in range(SSD_CONV):
            tap = xe if k == pad else pltpu.roll(xe, (pad - k) % rows, axis=0)
            acc = acc + cw_ref[k:k + 1, cols] * tap[SUBLANES:SUBLANES + t_len]
        xbc_ref[:, cols] = acc * jax.nn.sigmoid(acc)
    while others:
        other_dot(others.pop(0))


def _norm_inproj(x, g, w, conv_w, conv_b, seq_len):
    n = x.shape[0]
    widths = [hi - lo for lo, hi in zip(_INPROJ_SPLITS[:-1], _INPROJ_SPLITS[1:])]
    per = TOKEN_TILE // SUBLANES
    last = n // SUBLANES - 1
    halves = FNET_WIDTH // LANES
    out_shape = [jax.ShapeDtypeStruct((n, wd), F32) for wd in widths]
    out_specs = [pl.BlockSpec((TOKEN_TILE, wd), lambda i: (i, 0)) for wd in widths]
    out_shape[1] = jax.ShapeDtypeStruct((halves, n, LANES), F32)
    out_specs[1] = pl.BlockSpec((halves, TOKEN_TILE, LANES), lambda i: (0, i, 0))
    return pl.pallas_call(
        functools.partial(_norm_inproj_kernel, tiles_per_seq=seq_len // TOKEN_TILE),
        out_shape=out_shape,
        grid=(n // TOKEN_TILE,),
        in_specs=[pl.BlockSpec((SUBLANES, D_MODEL), lambda i: (jnp.maximum(i * per - 1, 0), 0)),
                  pl.BlockSpec((TOKEN_TILE, D_MODEL), lambda i: (i, 0)),
                  pl.BlockSpec((SUBLANES, D_MODEL), lambda i: (jnp.minimum((i + 1) * per, last), 0)),
                  _const_spec((1, D_MODEL)), _const_spec((D_MODEL, IN_PROJ_PAD)),
                  _const_spec(conv_w.shape), _const_spec(conv_b.shape)],
        out_specs=out_specs,
        compiler_params=_params(("parallel",)),
        name="norm_inproj",
    )(x, x, x, g, w, conv_w, conv_b)


def _s5_scan_kernel(uf_ref, ub_ref, bmat_ref, cmat_ref, lam_ref, yf_ref, yb_ref, bu0, bu1, h0, h1, carry):
    s = pl.program_id(1)
    nseq, t_len = uf_ref.shape[0], uf_ref.shape[1] // 2
    chains = [(i, d) for i in range(nseq) for d in range(2)]
    lam = [lam_ref[k] for k in range(4)]

    @pl.when(s == 0)
    def _():
        carry[...] = jnp.zeros_like(carry)
        bu1[...] = jnp.zeros_like(bu1)
        h0[...] = jnp.zeros_like(h0)

    def rows_of(half, d):
        lo = half * t_len if d == 0 else (1 - half) * t_len
        return pl.ds(lo, t_len)

    def project(half, bu, i, d):
        u16 = (uf_ref if d == 0 else ub_ref)[i, rows_of(half, d), :].astype(BF16)
        for part in range(2):
            col = (2 * d + part) * S5_NSTATE
            prod = _dot(u16, bmat_ref[:, col:col + S5_NSTATE])
            k = (i * 2 + d) * 2 + part
            for j in range(SUBLANES):
                bu[k, pl.ds(j, t_len, stride=SUBLANES), :] = prod[:, j * LANES:(j + 1) * LANES]

    def scan(bu, hb, hs, t_lo, t_hi):
        for t in range(t_lo, t_hi):
            for n, (i, d) in enumerate(chains):
                kre = (i * 2 + d) * 2
                row = pl.ds((t if d == 0 else t_len - 1 - t) * SUBLANES, SUBLANES)
                a_re, a_im = lam[2 * d], lam[2 * d + 1]
                h_re, h_im = hs[2 * n], hs[2 * n + 1]
                hs[2 * n] = a_re * h_re - a_im * h_im + bu[kre, row, :]
                hs[2 * n + 1] = a_re * h_im + a_im * h_re + bu[kre + 1, row, :]
                hb[kre, row, :] = hs[2 * n]
                hb[kre + 1, row, :] = hs[2 * n + 1]

    def read_out(half, hb, i, d):
        kre = (i * 2 + d) * 2
        h = jnp.concatenate(
            [hb[kre + part, pl.ds(j, t_len, stride=SUBLANES), :] for part in range(2) for j in range(SUBLANES)],
            axis=1).astype(BF16)
        (yf_ref if d == 0 else yb_ref)[i, rows_of(half, d), :] = _dot(h, cmat_ref[d])

    hs = [carry[k] for k in range(4 * nseq)]
    per = t_len // len(chains)
    for half, (bu_w, bu_r, h_w, h_r) in enumerate(((bu0, bu1, h1, h0), (bu1, bu0, h0, h1))):
        for n, (i, d) in enumerate(chains):
            project(half, bu_w, i, d)
            scan(bu_r, h_w, hs, n * per, (n + 1) * per)
            read_out(half, h_r, i, d)
    for k in range(4 * nseq):
        carry[k] = hs[k]


def _s5_scan(u, bmat, cmat, lam):
    b, l, _ = u.shape
    nb = l // (2 * S5_CHUNK)
    blk = (S5_SEQS, 2 * S5_CHUNK, S5_WIDTH)
    in_f = pl.BlockSpec(blk, lambda i, s: (i, jnp.minimum(s, nb - 1), 0))
    in_b = pl.BlockSpec(blk, lambda i, s: (i, nb - 1 - jnp.minimum(s, nb - 1), 0))
    out_f = pl.BlockSpec(blk, lambda i, s: (i, jnp.maximum(s - 1, 0), 0))
    out_b = pl.BlockSpec(blk, lambda i, s: (i, nb - 1 - jnp.maximum(s - 1, 0), 0))
    buf = pltpu.VMEM((S5_SEQS * 4, S5_CHUNK * SUBLANES, LANES), F32)
    return pl.pallas_call(
        _s5_scan_kernel,
        out_shape=[jax.ShapeDtypeStruct(u.shape, F32)] * 2,
        grid=(b // S5_SEQS, nb + 1),
        in_specs=[in_f, in_b, _const_spec(bmat.shape), _const_spec(cmat.shape), _const_spec(lam.shape)],
        out_specs=[out_f, out_b],
        scratch_shapes=[buf, buf, buf, buf, pltpu.VMEM((S5_SEQS * 4, SUBLANES, LANES), F32)],
        compiler_params=_params(("parallel", "arbitrary")),
        name="s5_scan",
    )(u, u, bmat, cmat, lam)


def _s5_prepare(b_re, b_im, dirs):
    g, p, c = S5_GROUPS, S5_STATE, S5_GROUP
    eye = jnp.eye(g, dtype=F32)
    bcols, cmats, lams = [], [], []
    for lam_re, lam_im, log_step, c_re, c_im in dirs:
        step = jnp.exp(log_step)[:, None]
        mag = jnp.exp(lam_re * step)
        lb_re, lb_im = mag * jnp.cos(lam_im * step), mag * jnp.sin(lam_im * step)
        den = lam_re * lam_re + lam_im * lam_im
        q_re = ((lb_re - 1.0) * lam_re + lb_im * lam_im) / den
        q_im = (lb_im * lam_re - (lb_re - 1.0) * lam_im) / den
        bb_re = q_re[..., None] * b_re - q_im[..., None] * b_im
        bb_im = q_re[..., None] * b_im + q_im[..., None] * b_re
        for bb in (bb_re, bb_im):
            bcols.append(jnp.einsum('gpc,gh->gchp', bb, eye).reshape(g * c, g * p))
        cm = [jnp.einsum('gcp,gh->gphc', cc, eye).reshape(g * p, g * c) for cc in (c_re, -c_im)]
        cmats.append(jnp.concatenate(cm, axis=0))
        lams += [lb_re.reshape(SUBLANES, LANES), lb_im.reshape(SUBLANES, LANES)]
    return (jnp.concatenate(bcols, axis=1).astype(BF16), jnp.stack(cmats).astype(BF16), jnp.stack(lams))


def _dft_parts(n, scale):
    k = np.arange(n)
    ang = 2.0 * np.pi * ((k[:, None] * k[None, :]) % n) / n
    return np.cos(ang) * scale, np.sin(ang) * scale


def _hilo(m):
    m = np.asarray(m, np.float32)
    hi = jnp.asarray(m, F32).astype(BF16)
    lo = (jnp.asarray(m, F32) - hi.astype(F32)).astype(BF16)
    return hi, lo


def _fnet_fold_kernel(ch_ref, cl_ref, sh_ref, sl_ref, w_ref, a_ref, b_ref):
    w = w_ref[...]
    a_ref[...] = _dot_hilo(ch_ref[...], cl_ref[...], w).astype(BF16)
    b_ref[...] = _dot_hilo(sh_ref[...], sl_ref[...], w).astype(BF16)


def _fnet_fold(w):
    c64, s64 = _dft_parts(FNET_GROUP, 1.0 / math.sqrt(FNET_GROUP))
    eye = np.eye(FNET_GROUPS)
    cbd, sbd = np.kron(eye, c64), np.kron(eye, s64)
    wbd = jnp.einsum('gcd,gh->gchd', w, jnp.eye(FNET_GROUPS, dtype=F32)).reshape(FNET_WIDTH, FNET_WIDTH)
    shape = (FNET_WIDTH, FNET_WIDTH)
    return pl.pallas_call(
        _fnet_fold_kernel,
        out_shape=[jax.ShapeDtypeStruct(shape, BF16)] * 2,
        name="fnet_fold",
    )(*_hilo(cbd), *_hilo(sbd), wbd)


def _fnet_stage1_kernel(x_ref, fh_ref, fl_ref, gre_ref, gim_ref, xs, gs):
    halves, l1, nb = x_ref.shape[0], x_ref.shape[1], x_ref.shape[2]
    for hf in range(halves):
        xs[hf] = x_ref[hf].reshape(l1 * nb, LANES)
    order = [(hf, s) for hf in range(halves) for s in range(nb)]
    g = _dot_hilo(fh_ref[...], fl_ref[...],
                  jnp.concatenate([xs[hf, pl.ds(s, l1, stride=nb), :] for hf, s in order], axis=1))
    for idx, (hf, s) in enumerate(order):
        gs[2 * hf, pl.ds(s, l1, stride=nb), :] = g[:l1, idx * LANES:(idx + 1) * LANES]
        gs[2 * hf + 1, pl.ds(s, l1, stride=nb), :] = g[l1:, idx * LANES:(idx + 1) * LANES]
    for hf in range(halves):
        gre_ref[hf] = gs[2 * hf].reshape(l1, nb, LANES)
        gim_ref[hf] = gs[2 * hf + 1].reshape(l1, nb, LANES)


def _fnet_stage2_kernel(gre_ref, gim_ref, twr_ref, twi_ref, f_ref, a_ref, b_ref, bias_ref, o_ref, os):
    halves, kb, l2 = gre_ref.shape[0], gre_ref.shape[1], gre_ref.shape[2]
    cols = []
    for i in range(kb):
        tr, ti = twr_ref[i], twi_ref[i]
        for hf in range(halves):
            gr, gi = gre_ref[hf, i], gim_ref[hf, i]
            cols.append(jnp.concatenate([gr * tr - gi * ti, gr * ti + gi * tr], axis=0))
    y = _dot(f_ref[...], jnp.concatenate(cols, axis=1).astype(BF16))
    y_re = jnp.concatenate([y[:l2, i * FNET_WIDTH:(i + 1) * FNET_WIDTH] for i in range(kb)], axis=0)
    y_im = jnp.concatenate([y[l2:, i * FNET_WIDTH:(i + 1) * FNET_WIDTH] for i in range(kb)], axis=0)
    out = _dot(y_re.astype(BF16), a_ref[...]) + _dot(y_im.astype(BF16), b_ref[...]) + bias_ref[...]
    for i in range(kb):
        for hf in range(halves):
            os[hf, pl.ds(i, l2, stride=kb), :] = out[i * l2:(i + 1) * l2, hf * LANES:(hf + 1) * LANES]
    for hf in range(halves):
        o_ref[hf] = os[hf].reshape(l2, kb, LANES)


def _fnet(v, b, a, bm, bias):
    halves, n, _ = v.shape
    l = n // b
    l2 = FNET_L2
    l1 = l // l2
    c1, s1 = _dft_parts(l1, 1.0 / math.sqrt(l1))
    f1h, f1l = _hilo(np.concatenate([c1, -s1], axis=0))
    c2, s2 = _dft_parts(l2, 1.0 / math.sqrt(l2))
    f2 = jnp.asarray(np.block([[c2, s2], [-s2, c2]]), F32).astype(BF16)
    m = (np.arange(l1)[:, None] * np.arange(l2)[None, :]) % l
    ang = 2.0 * np.pi * m / l
    twr = jnp.broadcast_to(jnp.asarray(np.cos(ang), F32)[:, :, None], (l1, l2, LANES))
    twi = jnp.broadcast_to(jnp.asarray(-np.sin(ang), F32)[:, :, None], (l1, l2, LANES))

    nb = FNET_STAGE1_ROWS // l1
    xs = pl.BlockSpec((halves, None, l1, nb, LANES), lambda i, j: (0, i, 0, j, 0))
    g_shape = jax.ShapeDtypeStruct((halves, b, l1, l2, LANES), F32)
    gre, gim = pl.pallas_call(
        _fnet_stage1_kernel,
        out_shape=[g_shape, g_shape],
        grid=(b, l2 // nb),
        in_specs=[xs, _const_spec(f1h.shape), _const_spec(f1l.shape)],
        out_specs=[xs, xs],
        scratch_shapes=[pltpu.VMEM((halves, l1 * nb, LANES), F32), pltpu.VMEM((2 * halves, l1 * nb, LANES), F32)],
        compiler_params=_params(("parallel", "parallel")),
        name="fnet_stage1",
    )(v.reshape(halves, b, l1, l2, LANES), f1h, f1l)

    kb = SUBLANES
    gs = pl.BlockSpec((halves, None, kb, l2, LANES), lambda i, j: (0, i, j, 0, 0))
    ts = pl.BlockSpec((kb, l2, LANES), lambda i, j: (j, 0, 0))
    y = pl.pallas_call(
        _fnet_stage2_kernel,
        out_shape=jax.ShapeDtypeStruct((halves, b, l2, l1, LANES), F32),
        grid=(b, l1 // kb),
        in_specs=[gs, gs, ts, ts, _const_spec(f2.shape),
                  _const_spec(a.shape), _const_spec(bm.shape), _const_spec(bias.shape)],
        out_specs=pl.BlockSpec((halves, None, l2, kb, LANES), lambda i, j: (0, i, 0, j, 0)),
        scratch_shapes=[pltpu.VMEM((halves, l2 * kb, LANES), F32)],
        compiler_params=_params(("parallel", "parallel")),
        name="fnet_stage2",
    )(gre, gim, twr, twi, f2, a, bm, bias)
    return y.reshape(halves, n, LANES)


def _ones_where(mask):
    return jnp.where(mask, 1.0, 0.0).astype(BF16)


def _softplus(x):
    return jnp.maximum(x, 0.0) + jnp.log1p(jnp.exp(-jnp.abs(x)))


def _ssd_dt_selector():
    sel = np.zeros((2, LANES, SSD_WIDTH), np.float32)
    for d in range(2):
        for lane in range(SSD_WIDTH):
            sel[d, 2 * d * SSD_HEADS + lane // SSD_HEAD_DIM, lane] = 1.0
    return jnp.asarray(sel, BF16)


def _ssd_scan_kernel(xf_ref, xb_ref, dtf_ref, dtb_ref, part_ref, sel_ref, yf_ref, yb_ref, state):
    c = pl.program_id(1)
    t_len = SSD_CHUNK
    n_sub = xf_ref.shape[1] // t_len
    pair_w = 2 * SSD_HEAD_DIM
    heads_per_pair = pair_w // SSD_HEAD_DIM
    n_pairs = SSD_WIDTH // pair_w
    pairs_per_group = n_pairs // SSD_GROUPS

    @pl.when(c == 0)
    def _():
        state[...] = jnp.zeros_like(state)

    row = lax.broadcasted_iota(jnp.int32, (t_len, t_len), 0)
    col = lax.broadcasted_iota(jnp.int32, (t_len, t_len), 1)
    lower, upper = _ones_where(row >= col), _ones_where(row <= col)
    first_head = lax.broadcasted_iota(jnp.int32, (1, pair_w), 1) < SSD_HEAD_DIM
    states = [[state[d, q] for q in range(n_pairs)] for d in range(2)]

    for k in range(n_sub):
        at = [k * t_len, (n_sub - 1 - k) * t_len]
        rows = []
        for d, dt_ref in enumerate((dtf_ref, dtb_ref)):
            dt_row = _softplus(dt_ref[0, at[d]:at[d] + t_len, :].T[0:SSD_HEADS] + part_ref[:, d:d + 1])
            rows += [dt_row, dt_row * (-LOG2_E * jnp.exp(part_ref[:, 2 + d:3 + d]))]
        n_rows = len(rows) * SSD_HEADS
        row_form = jnp.concatenate(rows, axis=0)
        col_form = jnp.concatenate([row_form, jnp.zeros((LANES - n_rows, t_len), F32)], axis=0).T
        cum_col = _dot_exact_lhs(lower, col_form)
        cum_row = _dot_exact_rhs(row_form, upper)
        cf_hi = col_form.astype(BF16)
        cf_lo = (col_form - cf_hi.astype(F32)).astype(BF16)

        for d, (x_ref, y_ref) in enumerate(((xf_ref, yf_ref), (xb_ref, yb_ref))):
            da_at = (2 * d + 1) * SSD_HEADS
            if d == 0:
                keep = row >= col
                p_col = cum_col
                p_row = cum_row[da_at:da_at + SSD_HEADS]
            else:
                keep = col >= row
                p_col = col_form - cum_col
                p_row = row_form[da_at:da_at + SSD_HEADS] - cum_row[da_at:da_at + SSD_HEADS]

            xbc = x_ref[0, at[d]:at[d] + t_len, :]
            dt_wide = _dot(cf_hi, sel_ref[d]) + _dot(cf_lo, sel_ref[d])
            for g in range(SSD_GROUPS):
                b_g = xbc[:, SSD_WIDTH + g * SSD_STATE:SSD_WIDTH + (g + 1) * SSD_STATE]
                c_lo = SSD_WIDTH + SSD_BC + g * SSD_STATE
                c_g = xbc[:, c_lo:c_lo + SSD_STATE].astype(BF16)
                bt_g = b_g.T.astype(BF16)
                cb = _dot(c_g, bt_g)
                group_pairs = range(g * pairs_per_group, (g + 1) * pairs_per_group)
                off_all = _dot(c_g, jnp.concatenate([states[d][q] for q in group_pairs], axis=1).astype(BF16))
                totals, weighted = [], []
                for j, q in enumerate(group_pairs):
                    h0 = q * heads_per_pair

                    def expand(m):
                        return jnp.where(first_head, m[:, da_at + h0:da_at + h0 + 1],
                                         m[:, da_at + h0 + 1:da_at + h0 + 2])

                    p_e = expand(p_col)
                    xdt = xbc[:, q * pair_w:(q + 1) * pair_w] * dt_wide[:, q * pair_w:(q + 1) * pair_w]
                    if d == 0:
                        total = p_e[t_len - 1:t_len]
                        off_scale = jnp.exp2(p_e)
                        w_state = jnp.exp2(total - p_e)
                    else:
                        total = expand(col_form[t_len - 1:t_len]) - p_e[t_len - 1:t_len]
                        off_scale = jnp.exp2(total + p_e)
                        w_state = jnp.exp2(-p_e)
                    y = off_all[:, j * pair_w:(j + 1) * pair_w] * off_scale
                    xdt16 = xdt.astype(BF16)
                    intra = []
                    for h in range(h0, h0 + heads_per_pair):
                        diff = p_col[:, da_at + h:da_at + h + 1] - p_row[h:h + 1, :]
                        m = cb * jnp.exp2(jnp.where(keep, diff, -jnp.inf))
                        intra.append(_dot(m.astype(BF16), xdt16))
                    y_ref[0, at[d]:at[d] + t_len, q * pair_w:(q + 1) * pair_w] = (
                        y + jnp.where(first_head, intra[0], intra[1]))
                    totals.append(total)
                    weighted.append((xdt * w_state).astype(BF16))
                update = _dot(bt_g, jnp.concatenate(weighted, axis=1))
                for j, q in enumerate(group_pairs):
                    states[d][q] = (states[d][q] * jnp.exp2(totals[j])
                                    + update[:, j * pair_w:(j + 1) * pair_w])

    for d in range(2):
        for q in range(n_pairs):
            state[d, q] = states[d][q]


def _ssd_scan(xbc, dt, part):
    b, l, ch = xbc.shape
    t_blk = SSD_CHUNK * SSD_SUB
    nb = l // t_blk
    pairs = SSD_WIDTH // (2 * SSD_HEAD_DIM)
    sel = _ssd_dt_selector()

    def fwd(width):
        return pl.BlockSpec((1, t_blk, width), lambda i, c: (i, c, 0))

    def bwd(width):
        return pl.BlockSpec((1, t_blk, width), lambda i, c: (i, nb - 1 - c, 0))

    return pl.pallas_call(
        _ssd_scan_kernel,
        out_shape=[jax.ShapeDtypeStruct((b, l, SSD_WIDTH), F32)] * 2,
        grid=(b, nb),
        in_specs=[fwd(ch), bwd(ch), fwd(DT_PAD), bwd(DT_PAD), _const_spec(part.shape), _const_spec(sel.shape)],
        out_specs=[fwd(SSD_WIDTH), bwd(SSD_WIDTH)],
        scratch_shapes=[pltpu.VMEM((2, pairs, SSD_STATE, 2 * SSD_HEAD_DIM), F32)],
        compiler_params=_params(("parallel", "arbitrary")),
        name="ssd_scan",
    )(xbc, xbc, dt, dt, part, sel)


def _mix_ffn_kernel(x_ref, s5f_ref, s5b_ref, u_ref, fn_ref, sdf_ref, sdb_ref, xc_ref, z_ref,
                    s5d_ref, wglu_ref, bglu_ref, sdd_ref, sdg_ref, wo_ref,
                    g_ref, wg_ref, wu_ref, wd_ref, gf_ref, o_ref, *, final_norm):
    ya = s5f_ref[...] + s5b_ref[...] + u_ref[...] * s5d_ref[...]
    ga = jax.nn.gelu(ya)
    ya = ga * jax.nn.sigmoid(_dot(ga.astype(BF16), wglu_ref[...]) + bglu_ref[...])
    z = z_ref[...]
    yc = (sdf_ref[...] + sdb_ref[...] + xc_ref[...] * sdd_ref[...]) * (z * jax.nn.sigmoid(z))
    yc = _rmsnorm(yc, sdg_ref[...])
    lo, mid = S5_WIDTH, S5_WIDTH + FNET_WIDTH
    x = (x_ref[...] + _dot(ya.astype(BF16), wo_ref[0:lo, :])
         + _dot(jnp.concatenate([fn_ref[hf] for hf in range(fn_ref.shape[0])], axis=1).astype(BF16), wo_ref[lo:mid, :])
         + _dot(yc.astype(BF16), wo_ref[mid:, :]))
    h = _rmsnorm(x, g_ref[...]).astype(BF16)
    gate = _dot(h, wg_ref[...])
    act = (gate * jax.nn.sigmoid(gate) * _dot(h, wu_ref[...])).astype(BF16)
    y = x + _dot(act, wd_ref[...])
    o_ref[...] = _rmsnorm(y, gf_ref[...]) if final_norm else y


def _mix_ffn(x, s5f, s5b, u, fn, sdf, sdb, xbc_c, z, consts, final_norm):
    n = x.shape[0]

    def rows(width):
        return pl.BlockSpec((FFN_TILE, width), lambda i: (i, 0))

    return pl.pallas_call(
        functools.partial(_mix_ffn_kernel, final_norm=final_norm),
        out_shape=jax.ShapeDtypeStruct(x.shape, F32),
        grid=(n // FFN_TILE,),
        in_specs=[rows(D_MODEL), rows(S5_WIDTH), rows(S5_WIDTH), rows(S5_WIDTH),
                  pl.BlockSpec((fn.shape[0], FFN_TILE, LANES), lambda i: (0, i, 0)),
                  rows(SSD_WIDTH), rows(SSD_WIDTH), rows(SSD_WIDTH), rows(SSD_WIDTH)]
                 + [_const_spec(a.shape) for a in consts],
        out_specs=rows(D_MODEL),
        compiler_params=_params(("parallel",)),
        name="mix_ffn",
    )(x, s5f, s5b, u, fn, sdf, sdb, xbc_c, z, *consts)


def _row(v):
    return v.reshape(1, -1).astype(F32)


def _prepare_layer(i, p):
    lay = {}
    lay['norm_mix_g'] = _row(p['norm_mix_g'][i])
    lay['w_in'] = jnp.pad(p['w_in'][i], ((0, 0), (0, IN_PROJ_PAD - p['w_in'].shape[-1]))).astype(BF16)
    lay['s5'] = _s5_prepare(
        p['s5_b_re'][i], p['s5_b_im'][i],
        [(p['s5_lam_re_f'][i], p['s5_lam_im_f'][i], p['s5_log_step_f'][i], p['s5_c_re_f'][i], p['s5_c_im_f'][i]),
         (p['s5_lam_re_b'][i], p['s5_lam_im_b'][i], p['s5_log_step_b'][i], p['s5_c_re_b'][i], p['s5_c_im_b'][i])])
    lay['s5_d'] = _row(p['s5_d'][i])
    lay['s5_w_glu'] = p['s5_w_glu'][i].astype(BF16)
    lay['s5_b_glu'] = _row(p['s5_b_glu'][i])
    lay['fnet_ab'] = _fnet_fold(p['fnet_w'][i])
    lay['fnet_b'] = _row(p['fnet_b'][i])
    lay['conv_w'] = p['ssd_conv_w'][i].astype(F32)
    lay['conv_b'] = _row(p['ssd_conv_b'][i])
    par = jnp.stack([p['ssd_dt_bias_f'][i], p['ssd_dt_bias_b'][i], p['ssd_a_log_f'][i], p['ssd_a_log_b'][i]]).astype(F32)
    lay['ssd_part'] = par.T
    lay['ssd_d'] = _row(jnp.repeat(p['ssd_d'][i], SSD_HEAD_DIM))
    lay['ssd_norm_g'] = _row(p['ssd_norm_g'][i])
    lay['w_out'] = p['w_out'][i].astype(BF16)
    lay['norm_ffn_g'] = _row(p['norm_ffn_g'][i])
    lay['w_gate'] = p['w_gate'][i].astype(BF16)
    lay['w_up'] = p['w_up'][i].astype(BF16)
    lay['w_down'] = p['w_down'][i].astype(BF16)
    return lay


def _trunk(x3, layers, final_g):
    b, l, dm = x3.shape
    x = x3.reshape(b * l, dm)
    for i, lay in enumerate(layers):
        u, v, z, xbc_c, dt = _norm_inproj(x, lay['norm_mix_g'], lay['w_in'], lay['conv_w'], lay['conv_b'], l)
        s5f, s5b = _s5_scan(u.reshape(b, l, -1), *lay['s5'])
        fn = _fnet(v, b, *lay['fnet_ab'], lay['fnet_b'])
        sdf, sdb = _ssd_scan(xbc_c.reshape(b, l, -1), dt.reshape(b, l, -1), lay['ssd_part'])
        n = b * l
        consts = (lay['s5_d'], lay['s5_w_glu'], lay['s5_b_glu'], lay['ssd_d'], lay['ssd_norm_g'], lay['w_out'],
                  lay['norm_ffn_g'], lay['w_gate'], lay['w_up'], lay['w_down'], final_g)
        x = _mix_ffn(x, s5f.reshape(n, -1), s5b.reshape(n, -1), u, fn,
                     sdf.reshape(n, -1), sdb.reshape(n, -1), xbc_c, z, consts,
                     final_norm=(i == len(layers) - 1))
    return x.reshape(b, l, dm)


def kernel(x_prompt, x_sample, norm_mix_g, w_in, s5_b_re, s5_b_im, s5_lam_re_f, s5_lam_im_f, s5_log_step_f, s5_c_re_f, s5_c_im_f, s5_lam_re_b, s5_lam_im_b, s5_log_step_b, s5_c_re_b, s5_c_im_b, s5_d, s5_w_glu, s5_b_glu, fnet_w, fnet_b, ssd_conv_w, ssd_conv_b, ssd_a_log_f, ssd_dt_bias_f, ssd_a_log_b, ssd_dt_bias_b, ssd_d, ssd_norm_g, w_out, norm_ffn_g, w_gate, w_up, w_down, final_norm_g):
    p = dict(norm_mix_g=norm_mix_g, w_in=w_in, s5_b_re=s5_b_re, s5_b_im=s5_b_im,
             s5_lam_re_f=s5_lam_re_f, s5_lam_im_f=s5_lam_im_f, s5_log_step_f=s5_log_step_f,
             s5_c_re_f=s5_c_re_f, s5_c_im_f=s5_c_im_f,
             s5_lam_re_b=s5_lam_re_b, s5_lam_im_b=s5_lam_im_b, s5_log_step_b=s5_log_step_b,
             s5_c_re_b=s5_c_re_b, s5_c_im_b=s5_c_im_b,
             s5_d=s5_d, s5_w_glu=s5_w_glu, s5_b_glu=s5_b_glu, fnet_w=fnet_w, fnet_b=fnet_b,
             ssd_conv_w=ssd_conv_w, ssd_conv_b=ssd_conv_b, ssd_a_log_f=ssd_a_log_f,
             ssd_dt_bias_f=ssd_dt_bias_f, ssd_a_log_b=ssd_a_log_b, ssd_dt_bias_b=ssd_dt_bias_b,
             ssd_d=ssd_d, ssd_norm_g=ssd_norm_g, w_out=w_out, norm_ffn_g=norm_ffn_g,
             w_gate=w_gate, w_up=w_up, w_down=w_down)
    layers = [_prepare_layer(i, p) for i in range(norm_mix_g.shape[0])]
    final_g = _row(final_norm_g)
    return (_trunk(x_prompt, layers, final_g), _trunk(x_sample, layers, final_g))
```
